```python
import jax, jax.numpy as jnp
from jax import lax
import numpy as np

D_MODEL = 1024
BATCH = 8
SEQ = 4096
DEPTH = 2

N_MEM = 256
MIX_WIDTH = 2 * D_MODEL
XA_HEADS = 4
XA_WIDTH = MIX_WIDTH // 4
XA_HEAD_DIM = XA_WIDTH // XA_HEADS
BRANCH_WIDTH = (MIX_WIDTH - XA_WIDTH) // 2
CHUNK = 128
A_HEADS = 4
A_HEAD_DIM = BRANCH_WIDTH // A_HEADS
SHORT_CONV = 3
POOL_WINDOWS = (2, 4, 8, 16)
C_GROUP = BRANCH_WIDTH // len(POOL_WINDOWS)
CONF_CONV = 31
EPS = 1e-6

N_EVEN = (DEPTH + 1) // 2
N_ODD = DEPTH // 2
EVEN_SPLITS = (BRANCH_WIDTH, BRANCH_WIDTH, BRANCH_WIDTH, BRANCH_WIDTH, BRANCH_WIDTH, XA_WIDTH, MIX_WIDTH)
ODD_SPLITS = (BRANCH_WIDTH, BRANCH_WIDTH, BRANCH_WIDTH, XA_WIDTH, MIX_WIDTH)
EVEN_IN = sum(EVEN_SPLITS)
ODD_IN = sum(ODD_SPLITS)

kernel_name = "hybrid_sgu_shortconv_pool_conformer_memxattn"


def _split(p, sizes):
    idx = [int(v) for v in np.cumsum(sizes)[:-1]]
    return jnp.split(p, idx, axis=-1)


def rms_norm(x, g):
    xf = x.astype(jnp.float32)
    y = xf * lax.rsqrt(jnp.mean(xf * xf, axis=-1, keepdims=True) + EPS)
    return (y * g.astype(jnp.float32)).astype(x.dtype)


def layer_norm(x, g, b):
    xf = x.astype(jnp.float32)
    mu = jnp.mean(xf, axis=-1, keepdims=True)
    var = jnp.mean(jnp.square(xf - mu), axis=-1, keepdims=True)
    y = (xf - mu) * lax.rsqrt(var + EPS)
    return (y * g.astype(jnp.float32) + b.astype(jnp.float32)).astype(x.dtype)


def causal_dwconv(x, w):
    k = w.shape[0]
    return lax.conv_general_dilated(
        x, w[:, None, :].astype(x.dtype), window_strides=(1,), padding=[(k - 1, 0)],
        dimension_numbers=("NWC", "WIO", "NWC"), feature_group_count=x.shape[-1])


def memory_cross_attention(q, mem_n, w_kv):
    bsz, s, _ = q.shape
    k, v = _split(jnp.einsum("bmd,de->bme", mem_n, w_kv), (XA_WIDTH, XA_WIDTH))
    q = q.reshape(bsz, s, XA_HEADS, XA_HEAD_DIM)
    k = k.reshape(bsz, -1, XA_HEADS, XA_HEAD_DIM)
    v = v.reshape(bsz, -1, XA_HEADS, XA_HEAD_DIM)
    scores = jnp.einsum("bshd,bmhd->bhsm", q, k).astype(jnp.float32) * (XA_HEAD_DIM ** -0.5)
    probs = jax.nn.softmax(scores, axis=-1).astype(v.dtype)
    return jnp.einsum("bhsm,bmhd->bshd", probs, v).reshape(bsz, s, XA_WIDTH)


def spatial_gating(u, v, ln_g, ln_b, w_s, b_s):
    bsz, s, _ = v.shape
    v = layer_norm(v, ln_g, ln_b).reshape(bsz, s // CHUNK, CHUNK, A_HEADS, A_HEAD_DIM)
    causal = jnp.tril(jnp.ones((CHUNK, CHUNK), dtype=bool))
    w = jnp.where(causal[None], w_s, 0.0).astype(v.dtype)
    sg = jnp.einsum("hts,bnshd->bnthd", w, v) + b_s.T[None, None, :, :, None]
    return u * sg.reshape(bsz, s, BRANCH_WIDTH)


def short_gated_conv(bg, cg, xin, w_conv):
    return bg * causal_dwconv(cg * xin, w_conv)


def multiscale_pool(z, w_grp, scale):
    s = z.shape[1]
    zf = z.astype(jnp.float32)
    csum = jnp.pad(jnp.cumsum(zf, axis=1), ((0, 0), (1, 0), (0, 0)))
    t = jnp.arange(1, s + 1)
    outs = []
    for g, win in enumerate(POOL_WINDOWS):
        sl = slice(g * C_GROUP, (g + 1) * C_GROUP)
        c = csum[..., sl]
        lo = jnp.pad(c, ((0, 0), (win, 0), (0, 0)))[:, : s + 1]
        cnt = jnp.minimum(t, win).astype(jnp.float32)[None, :, None]
        pooled = (c[:, 1:] - lo[:, 1:]) / cnt - zf[..., sl]
        outs.append(jnp.einsum("bsc,cd->bsd", pooled.astype(z.dtype), w_grp[g]))
    return jnp.concatenate(outs, axis=-1) * scale


def conformer_conv(a, b, w_dw, b_dw, ln_g, ln_b, w_pw, b_pw):
    z = a * jax.nn.sigmoid(b)
    z = causal_dwconv(z, w_dw) + b_dw
    z = jax.nn.silu(layer_norm(z, ln_g, ln_b))
    return jnp.einsum("bsc,cd->bsd", z, w_pw) + b_pw


def even_layer(x, mem, pre_g, w_in, a_ln_g, a_ln_b, a_ws, a_bs, b_conv, mem_g, w_kv, w_out, post_g):
    h = rms_norm(x, pre_g)
    p = jnp.einsum("bsd,de->bse", h, w_in)
    u, v, bg, cg, xin, q, gate = _split(p, EVEN_SPLITS)
    ya = spatial_gating(u, v, a_ln_g, a_ln_b, a_ws, a_bs)
    yb = short_gated_conv(bg, cg, xin, b_conv)
    yx = memory_cross_attention(q, rms_norm(mem, mem_g), w_kv)
    y = jnp.concatenate([ya, yb, yx], axis=-1) * jax.nn.silu(gate)
    return x + rms_norm(jnp.einsum("bse,ed->bsd", y, w_out), post_g)


def odd_layer(x, mem, pre_g, w_in, c_wgrp, c_scale, d_dw_w, d_dw_b, d_ln_g, d_ln_b, d_pw_w, d_pw_b,
              mem_g, w_kv, w_out, post_g):
    h = rms_norm(x, pre_g)
    p = jnp.einsum("bsd,de->bse", h, w_in)
    zc, ga, gb, q, gate = _split(p, ODD_SPLITS)
    yc = multiscale_pool(zc, c_wgrp, c_scale)
    yd = conformer_conv(ga, gb, d_dw_w, d_dw_b, d_ln_g, d_ln_b, d_pw_w, d_pw_b)
    yx = memory_cross_attention(q, rms_norm(mem, mem_g), w_kv)
    y = jnp.concatenate([yc, yd, yx], axis=-1) * jax.nn.silu(gate)
    return x + rms_norm(jnp.einsum("bse,ed->bsd", y, w_out), post_g)


def setup_inputs(seed: int = 0) -> dict:
    key = jax.random.key(seed)
    ks = iter(jax.random.split(key, 40))
    f32 = jnp.float32

    def nrm(shape, scale):
        return jax.random.normal(next(ks), shape, f32) * scale

    def gain(shape):
        return 1.0 + 0.05 * jax.random.normal(next(ks), shape, f32)

    ne, no = N_EVEN, N_ODD
    bw = BRANCH_WIDTH
    return {
        "x": jax.random.normal(next(ks), (BATCH, SEQ, D_MODEL), f32),
        "mem": jax.random.normal(next(ks), (BATCH, N_MEM, D_MODEL), f32),
        "even_pre_g": gain((ne, D_MODEL)),
        "even_w_in": nrm((ne, D_MODEL, EVEN_IN), D_MODEL ** -0.5),
        "even_a_ln_g": gain((ne, bw)),
        "even_a_ln_b": nrm((ne, bw), 0.02),
        "even_a_ws": nrm((ne, A_HEADS, CHUNK, CHUNK), CHUNK ** -0.5),
        "even_a_bs": nrm((ne, A_HEADS, CHUNK), 0.02),
        "even_b_conv": nrm((ne, SHORT_CONV, bw), SHORT_CONV ** -0.5),
        "even_mem_g": gain((ne, D_MODEL)),
        "even_w_kv": nrm((ne, D_MODEL, 2 * XA_WIDTH), D_MODEL ** -0.5),
        "even_w_out": nrm((ne, MIX_WIDTH, D_MODEL), MIX_WIDTH ** -0.5),
        "even_post_g": gain((ne, D_MODEL)),
        "odd_pre_g": gain((no, D_MODEL)),
        "odd_w_in": nrm((no, D_MODEL, ODD_IN), D_MODEL ** -0.5),
        "odd_c_wgrp": nrm((no, len(POOL_WINDOWS), C_GROUP, C_GROUP), C_GROUP ** -0.5),
        "odd_c_scale": gain((no, bw)),
        "odd_d_dw_w": nrm((no, CONF_CONV, bw), CONF_CONV ** -0.5),
        "odd_d_dw_b": nrm((no, bw), 0.02),
        "odd_d_ln_g": gain((no, bw)),
        "odd_d_ln_b": nrm((no, bw), 0.02),
        "odd_d_pw_w": nrm((no, bw, bw), bw ** -0.5),
        "odd_d_pw_b": nrm((no, bw), 0.02),
        "odd_mem_g": gain((no, D_MODEL)),
        "odd_w_kv": nrm((no, D_MODEL, 2 * XA_WIDTH), D_MODEL ** -0.5),
        "odd_w_out": nrm((no, MIX_WIDTH, D_MODEL), MIX_WIDTH ** -0.5),
        "odd_post_g": gain((no, D_MODEL)),
    }


def reference(x, mem,
              even_pre_g, even_w_in, even_a_ln_g, even_a_ln_b, even_a_ws, even_a_bs, even_b_conv,
              even_mem_g, even_w_kv, even_w_out, even_post_g,
              odd_pre_g, odd_w_in, odd_c_wgrp, odd_c_scale, odd_d_dw_w, odd_d_dw_b, odd_d_ln_g,
              odd_d_ln_b, odd_d_pw_w, odd_d_pw_b, odd_mem_g, odd_w_kv, odd_w_out, odd_post_g):
    for layer in range(DEPTH):
        i = layer // 2
        if layer % 2 == 0:
            x = even_layer(x, mem, even_pre_g[i], even_w_in[i], even_a_ln_g[i], even_a_ln_b[i],
                           even_a_ws[i], even_a_bs[i], even_b_conv[i], even_mem_g[i], even_w_kv[i],
                           even_w_out[i], even_post_g[i])
        else:
            x = odd_layer(x, mem, odd_pre_g[i], odd_w_in[i], odd_c_wgrp[i], odd_c_scale[i],
                          odd_d_dw_w[i], odd_d_dw_b[i], odd_d_ln_g[i], odd_d_ln_b[i], odd_d_pw_w[i],
                          odd_d_pw_b[i], odd_mem_g[i], odd_w_kv[i], odd_w_out[i], odd_post_g[i])
    return x
```

```python
import functools

import jax
import jax.numpy as jnp
from jax import lax
from jax.experimental import pallas as pl
from jax.experimental.pallas import tpu as pltpu

D_MODEL = 1024
N_MEM = 256
MIX_WIDTH = 2 * D_MODEL
XA_HEADS = 4
XA_WIDTH = MIX_WIDTH // 4
XA_HEAD_DIM = XA_WIDTH // XA_HEADS
BRANCH_WIDTH = (MIX_WIDTH - XA_WIDTH) // 2
CHUNK = 128
A_HEADS = 4
A_HEAD_DIM = BRANCH_WIDTH // A_HEADS
SHORT_CONV = 3
POOL_WINDOWS = (2, 4, 8, 16)
C_GROUP = BRANCH_WIDTH // len(POOL_WINDOWS)
CONF_CONV = 31
EPS = 1e-6

LANES = 128
SUBLANES = 8
N_SLABS = BRANCH_WIDTH // LANES
SEQ_TILE = 256
VMEM_LIMIT_BYTES = 56 * 1024 * 1024

B_HALO = SUBLANES
C_HALO = 2 * SUBLANES
D_HALO = 4 * SUBLANES

BF16 = jnp.bfloat16
F32 = jnp.float32


def _dot(a, b):
    return jnp.dot(a, b, preferred_element_type=F32)


def _rms_norm(x, g):
    y = x * lax.rsqrt(jnp.mean(x * x, axis=-1, keepdims=True) + EPS)
    return y * g


def _layer_norm(x, g, b):
    mu = jnp.mean(x, axis=-1, keepdims=True)
    xc = x - mu
    var = jnp.mean(xc * xc, axis=-1, keepdims=True)
    return xc * lax.rsqrt(var + EPS) * g + b


def _slab(v, s):
    return v[:, s * LANES:(s + 1) * LANES]


def _cross_attention(q, kt_ref, v_ref):
    qb = q.astype(BF16)
    outs = []
    for hd in range(XA_HEADS):
        lo, hi = hd * XA_HEAD_DIM, (hd + 1) * XA_HEAD_DIM
        s = _dot(qb[:, lo:hi], kt_ref[0, lo:hi, :]) * (XA_HEAD_DIM ** -0.5)
        m = jnp.max(s, axis=-1, keepdims=True)
        e = jnp.exp(s - m)
        p = e / jnp.sum(e, axis=-1, keepdims=True)
        outs.append(_dot(p.astype(BF16), v_ref[0, :, lo:hi]))
    return jnp.concatenate(outs, axis=-1)


def _carry_halo(ext_ref, halo, tile):
    for s in range(N_SLABS):
        ext_ref[s, 0:halo, :] = ext_ref[s, tile:tile + halo, :]


def _zero_halo(ext_ref, halo):
    ext_ref[:, 0:halo, :] = jnp.zeros((N_SLABS, halo, LANES), F32)


def _even_kernel(x_ref, kt_ref, v_ref, pre_g_ref, w_in_ref, ln_g_ref, ln_b_ref, ws_ref, bs_ref,
                 bconv_ref, w_out_ref, post_g_ref, o_ref, bext_ref):
    tile = x_ref.shape[1]
    bw = BRANCH_WIDTH

    @pl.when(pl.program_id(1) == 0)
    def _():
        _zero_halo(bext_ref, B_HALO)

    x = x_ref[0]
    h = _rms_norm(x, pre_g_ref[...]).astype(BF16)

    def proj(lo, width):
        return _dot(h, w_in_ref[:, lo:lo + width])

    u = proj(0, bw)
    v = proj(bw, bw)
    vn = _layer_norm(v, ln_g_ref[...], ln_b_ref[...]).astype(BF16)
    low_half = lax.broadcasted_iota(jnp.int32, (CHUNK, LANES), 1) < (LANES // 2)
    sg_chunks = []
    for c in range(tile // CHUNK):
        rows = vn[c * CHUNK:(c + 1) * CHUNK, :]
        r0 = _dot(ws_ref[0], rows[:, 0:256])
        r1 = _dot(ws_ref[1], rows[:, 128:384])
        r2 = _dot(ws_ref[2], rows[:, 384:640])
        r3 = _dot(ws_ref[3], rows[:, 512:768])
        sg_chunks.append(jnp.concatenate([
            r0[:, 0:128], jnp.where(low_half, r0[:, 128:256], r1[:, 0:128]), r1[:, 128:256],
            r2[:, 0:128], jnp.where(low_half, r2[:, 128:256], r3[:, 0:128]), r3[:, 128:256]],
            axis=-1) + bs_ref[...])
    ya = u * jnp.concatenate(sg_chunks, axis=0)

    bg = proj(2 * bw, bw)
    cx = proj(3 * bw, bw) * proj(4 * bw, bw)
    for s in range(N_SLABS):
        bext_ref[s, B_HALO:B_HALO + tile, :] = _slab(cx, s)
    conv = []
    for s in range(N_SLABS):
        acc = None
        for k in range(SHORT_CONV):
            term = bext_ref[s, pl.ds(B_HALO - (SHORT_CONV - 1) + k, tile), :] * bconv_ref[k:k + 1, s * LANES:(s + 1) * LANES]
            acc = term if acc is None else acc + term
        conv.append(acc)
    yb = bg * jnp.concatenate(conv, axis=-1)
    _carry_halo(bext_ref, B_HALO, tile)

    yx = _cross_attention(proj(5 * bw, XA_WIDTH), kt_ref, v_ref)

    g0 = 5 * bw + XA_WIDTH
    out = (_dot((ya * jax.nn.silu(proj(g0, bw))).astype(BF16), w_out_ref[0:bw, :])
           + _dot((yb * jax.nn.silu(proj(g0 + bw, bw))).astype(BF16), w_out_ref[bw:2 * bw, :])
           + _dot((yx * jax.nn.silu(proj(g0 + 2 * bw, XA_WIDTH))).astype(BF16), w_out_ref[2 * bw:, :]))
    o_ref[0] = x + _rms_norm(out, post_g_ref[...])


def _window_sums(ext_ref, s, tile, windows):
    sums = {}
    acc = ext_ref[s, pl.ds(C_HALO, tile), :]
    for j in range(1, max(windows)):
        acc = acc + ext_ref[s, pl.ds(C_HALO - j, tile), :]
        if j + 1 in windows:
            sums[j + 1] = acc
    return sums


_SLAB_WINDOWS = ((2,), (2, 4), (4,), (8,), (8, 16), (16,))


def _odd_kernel(x_ref, kt_ref, v_ref, inv_cnt_ref, pre_g_ref, w_in_ref, wgrp_ref, c_scale_ref,
                dw_w_ref, dw_b_ref, ln_g_ref, ln_b_ref, pw_w_ref, pw_b_ref, w_out_ref, post_g_ref,
                o_ref, cext_ref, zext_ref):
    tile = x_ref.shape[1]
    bw = BRANCH_WIDTH

    @pl.when(pl.program_id(1) == 0)
    def _():
        _zero_halo(cext_ref, C_HALO)
        _zero_halo(zext_ref, D_HALO)

    x = x_ref[0]
    h = _rms_norm(x, pre_g_ref[...]).astype(BF16)

    def proj(lo, width):
        return _dot(h, w_in_ref[:, lo:lo + width])

    zc = proj(0, bw)
    for s in range(N_SLABS):
        cext_ref[s, C_HALO:C_HALO + tile, :] = _slab(zc, s)
    low_half = lax.broadcasted_iota(jnp.int32, (tile, LANES), 1) < (LANES // 2)
    pooled = []
    for s in range(N_SLABS):
        sums = _window_sums(cext_ref, s, tile, _SLAB_WINDOWS[s])
        wins = _SLAB_WINDOWS[s]
        tot = sums[wins[0]] if len(wins) == 1 else jnp.where(low_half, sums[wins[0]], sums[wins[1]])
        pooled.append(tot * inv_cnt_ref[0, :, s * LANES:(s + 1) * LANES] - _slab(zc, s))
    _carry_halo(cext_ref, C_HALO, tile)
    pooled = jnp.concatenate(pooled, axis=-1).astype(BF16)
    half = bw // 2
    yc = jnp.concatenate([_dot(pooled[:, :half], wgrp_ref[0]), _dot(pooled[:, half:], wgrp_ref[1])],
                         axis=-1) * c_scale_ref[...]

    z = proj(bw, bw) * jax.nn.sigmoid(proj(2 * bw, bw))
    for s in range(N_SLABS):
        zext_ref[s, D_HALO:D_HALO + tile, :] = _slab(z, s)
    conv = []
    for s in range(N_SLABS):
        acc = None
        for k in range(CONF_CONV):
            term = zext_ref[s, pl.ds(D_HALO - (CONF_CONV - 1) + k, tile), :] * dw_w_ref[k:k + 1, s * LANES:(s + 1) * LANES]
            acc = term if acc is None else acc + term
        conv.append(acc)
    _carry_halo(zext_ref, D_HALO, tile)
    zd = jnp.concatenate(conv, axis=-1) + dw_b_ref[...]
    zd = jax.nn.silu(_layer_norm(zd, ln_g_ref[...], ln_b_ref[...]))
    yd = _dot(zd.astype(BF16), pw_w_ref[...]) + pw_b_ref[...]

    yx = _cross_attention(proj(3 * bw, XA_WIDTH), kt_ref, v_ref)

    g0 = 3 * bw + XA_WIDTH
    out = (_dot((yc * jax.nn.silu(proj(g0, bw))).astype(BF16), w_out_ref[0:bw, :])
           + _dot((yd * jax.nn.silu(proj(g0 + bw, bw))).astype(BF16), w_out_ref[bw:2 * bw, :])
           + _dot((yx * jax.nn.silu(proj(g0 + 2 * bw, XA_WIDTH))).astype(BF16), w_out_ref[2 * bw:, :]))
    o_ref[0] = x + _rms_norm(out, post_g_ref[...])


def _kv_kernel(mem_ref, g_ref, wkt_ref, wv_ref, kt_ref, v_ref):
    mem_n = _rms_norm(mem_ref[0], g_ref[...]).astype(BF16)
    kt = lax.dot_general(wkt_ref[...], mem_n, (((1,), (1,)), ((), ())), preferred_element_type=F32)
    kt_ref[0] = kt.astype(BF16)
    v_ref[0] = _dot(mem_n, wv_ref[...]).astype(BF16)


def _memory_kv(mem, mem_g, w_kv):
    bsz = mem.shape[0]
    wkt = w_kv[:, :XA_WIDTH].T.astype(BF16)
    wv = w_kv[:, XA_WIDTH:].astype(BF16)
    const = lambda b: (0, 0)
    return pl.pallas_call(
        _kv_kernel,
        grid=(bsz,),
        in_specs=[pl.BlockSpec((1, N_MEM, D_MODEL), lambda b: (b, 0, 0)),
                  pl.BlockSpec((1, D_MODEL), const),
                  pl.BlockSpec((XA_WIDTH, D_MODEL), const),
                  pl.BlockSpec((D_MODEL, XA_WIDTH), const)],
        out_specs=[pl.BlockSpec((1, XA_WIDTH, N_MEM), lambda b: (b, 0, 0)),
                   pl.BlockSpec((1, N_MEM, XA_WIDTH), lambda b: (b, 0, 0))],
        out_shape=[jax.ShapeDtypeStruct((bsz, XA_WIDTH, N_MEM), BF16),
                   jax.ShapeDtypeStruct((bsz, N_MEM, XA_WIDTH), BF16)],
        compiler_params=pltpu.CompilerParams(dimension_semantics=("arbitrary",)),
        name="memory_kv",
    )(mem, mem_g.reshape(1, D_MODEL), wkt, wv)


def _resident(shape):
    zeros = (0,) * len(shape)
    return pl.BlockSpec(shape, lambda b, j: zeros, pipeline_mode=pl.Buffered(1))


def _row(p):
    return p.reshape(1, -1)


def _layer_call(body, name, x, kt, v, extra_specs, extra_args, resident_args, scratch_shapes):
    bsz, seq, d = x.shape
    tile = SEQ_TILE
    assert seq % tile == 0 and tile % CHUNK == 0
    tile_spec = pl.BlockSpec((1, tile, d), lambda b, j: (b, j, 0))
    in_specs = ([tile_spec,
                 pl.BlockSpec((1, XA_WIDTH, N_MEM), lambda b, j: (b, 0, 0)),
                 pl.BlockSpec((1, N_MEM, XA_WIDTH), lambda b, j: (b, 0, 0))]
                + extra_specs + [_resident(a.shape) for a in resident_args])
    return pl.pallas_call(
        body,
        grid=(bsz, seq // tile),
        in_specs=in_specs,
        out_specs=tile_spec,
        out_shape=jax.ShapeDtypeStruct(x.shape, x.dtype),
        scratch_shapes=scratch_shapes,
        compiler_params=pltpu.CompilerParams(dimension_semantics=("arbitrary", "arbitrary"),
                                             vmem_limit_bytes=VMEM_LIMIT_BYTES),
        name=name,
    )(x, kt, v, *extra_args, *resident_args)


def _even_layer(x, mem, pre_g, w_in, a_ln_g, a_ln_b, a_ws, a_bs, b_conv, mem_g, w_kv, w_out, post_g):
    kt, v = _memory_kv(mem, mem_g, w_kv)
    causal = jnp.tril(jnp.ones((CHUNK, CHUNK), dtype=bool))
    ws = jnp.where(causal[None], a_ws, 0.0).astype(BF16)
    bs = jnp.repeat(a_bs.T, A_HEAD_DIM, axis=1)
    resident = [_row(pre_g), w_in.astype(BF16), _row(a_ln_g), _row(a_ln_b), ws, bs, b_conv,
                w_out.astype(BF16), _row(post_g)]
    scratch = [pltpu.VMEM((N_SLABS, B_HALO + SEQ_TILE, LANES), F32)]
    return _layer_call(_even_kernel, "even_layer", x, kt, v, [], [], resident, scratch)


def _odd_layer(x, mem, pre_g, w_in, c_wgrp, c_scale, d_dw_w, d_dw_b, d_ln_g, d_ln_b, d_pw_w, d_pw_b,
               mem_g, w_kv, w_out, post_g):
    kt, v = _memory_kv(mem, mem_g, w_kv)
    tile = SEQ_TILE
    win = jnp.repeat(jnp.asarray(POOL_WINDOWS, F32), C_GROUP)
    pos = jnp.arange(1, tile + 1, dtype=F32)[:, None]
    inv_cnt = jnp.stack([1.0 / jnp.minimum(pos, win[None, :]),
                         jnp.broadcast_to(1.0 / win[None, :], (tile, BRANCH_WIDTH))])
    zero = jnp.zeros((C_GROUP, C_GROUP), c_wgrp.dtype)
    wgrp = jnp.stack([jnp.block([[c_wgrp[0], zero], [zero, c_wgrp[1]]]),
                      jnp.block([[c_wgrp[2], zero], [zero, c_wgrp[3]]])]).astype(BF16)
    extra_specs = [pl.BlockSpec((1, tile, BRANCH_WIDTH), lambda b, j: (jnp.minimum(j, 1), 0, 0))]
    resident = [_row(pre_g), w_in.astype(BF16), wgrp, _row(c_scale), d_dw_w, _row(d_dw_b),
                _row(d_ln_g), _row(d_ln_b), d_pw_w.astype(BF16), _row(d_pw_b), w_out.astype(BF16),
                _row(post_g)]
    scratch = [pltpu.VMEM((N_SLABS, C_HALO + tile, LANES), F32),
               pltpu.VMEM((N_SLABS, D_HALO + tile, LANES), F32)]
    return _layer_call(_odd_kernel, "odd_layer", x, kt, v, extra_specs, [inv_cnt], resident, scratch)


def kernel(x, mem, even_pre_g, even_w_in, even_a_ln_g, even_a_ln_b, even_a_ws, even_a_bs, even_b_conv, even_mem_g, even_w_kv, even_w_out, even_post_g, odd_pre_g, odd_w_in, odd_c_wgrp, odd_c_scale, odd_d_dw_w, odd_d_dw_b, odd_d_ln_g, odd_d_ln_b, odd_d_pw_w, odd_d_pw_b, odd_mem_g, odd_w_kv, odd_w_out, odd_post_g):
    depth = even_pre_g.shape[0] + odd_pre_g.shape[0]
    for layer in range(depth):
        i = layer // 2
        if layer % 2 == 0:
            x = _even_layer(x, mem, even_pre_g[i], even_w_in[i], even_a_ln_g[i], even_a_ln_b[i],
                            even_a_ws[i], even_a_bs[i], even_b_conv[i], even_mem_g[i], even_w_kv[i],
                            even_w_out[i], even_post_g[i])
        else:
            x = _odd_layer(x, mem, odd_pre_g[i], odd_w_in[i], odd_c_wgrp[i], odd_c_scale[i],
                           odd_d_dw_w[i], odd_d_dw_b[i], odd_d_ln_g[i], odd_d_ln_b[i], odd_d_pw_w[i],
                           odd_d_pw_b[i], odd_mem_g[i], odd_w_kv[i], odd_w_out[i], odd_post_g[i])
    return x
```

```python
import functools

import jax
import jax.numpy as jnp
from jax import lax
from jax.experimental import pallas as pl
from jax.experimental.pallas import tpu as pltpu

D_MODEL = 1024
N_MEM = 256
MIX_WIDTH = 2 * D_MODEL
XA_HEADS = 4
XA_WIDTH = MIX_WIDTH // 4
XA_HEAD_DIM = XA_WIDTH // XA_HEADS
BRANCH_WIDTH = (MIX_WIDTH - XA_WIDTH) // 2
CHUNK = 128
A_HEADS = 4
A_HEAD_DIM = BRANCH_WIDTH // A_HEADS
SHORT_CONV = 3
POOL_WINDOWS = (2, 4, 8, 16)
C_GROUP = BRANCH_WIDTH // len(POOL_WINDOWS)
CONF_CONV = 31
EPS = 1e-6
EVEN_IN = 5 * BRANCH_WIDTH + XA_WIDTH + MIX_WIDTH
ODD_IN = 3 * BRANCH_WIDTH + XA_WIDTH + MIX_WIDTH

LANES = 128
SUBLANES = 8
MXU_WIDTH = 256
N_SLABS = BRANCH_WIDTH // LANES
SEQ_TILE = 256
PIPELINE_LAG = 2
PROJ_PIECE = 3 * MXU_WIDTH
FENCE_ROWS = 2 * SUBLANES
VMEM_LIMIT_BYTES = 56 * 1024 * 1024

B_HALO = SUBLANES
C_HALO = 2 * SUBLANES
D_HALO = 4 * SUBLANES

BF16 = jnp.bfloat16
F32 = jnp.float32


def _dot(a, b):
    return jnp.dot(a, b, preferred_element_type=F32)


def _rms_norm(x, g):
    y = x * lax.rsqrt(jnp.mean(x * x, axis=-1, keepdims=True) + EPS)
    return y * g


def _layer_norm(x, g, b):
    mu = jnp.mean(x, axis=-1, keepdims=True)
    xc = x - mu
    var = jnp.mean(xc * xc, axis=-1, keepdims=True)
    return xc * lax.rsqrt(var + EPS) * g + b


def _slab(v, s):
    return v[:, s * LANES:(s + 1) * LANES]


def _cross_attention(q, kt_ref, v_ref):
    qb = q.astype(BF16)
    outs = []
    for hd in range(XA_HEADS):
        lo, hi = hd * XA_HEAD_DIM, (hd + 1) * XA_HEAD_DIM
        s = _dot(qb[:, lo:hi], kt_ref[0, lo:hi, :]) * (XA_HEAD_DIM ** -0.5)
        m = jnp.max(s, axis=-1, keepdims=True)
        e = jnp.exp(s - m)
        p = e / jnp.sum(e, axis=-1, keepdims=True)
        outs.append(_dot(p.astype(BF16), v_ref[0, :, lo:hi]))
    return jnp.concatenate(outs, axis=-1)


def _causal_dwconv_slab(ext_ref, w_ref, s, halo, taps, tile):
    acc = None
    for k in range(taps):
        term = ext_ref[s, pl.ds(halo - (taps - 1) + k, tile), :] * w_ref[k:k + 1, s * LANES:(s + 1) * LANES]
        acc = term if acc is None else acc + term
    return acc


def _carry_halo(ext_ref, halo, tile):
    for s in range(N_SLABS):
        ext_ref[s, 0:halo, :] = ext_ref[s, tile:tile + halo, :]


def _zero_halo(ext_ref, halo):
    ext_ref[:, 0:halo, :] = jnp.zeros((N_SLABS, halo, LANES), F32)


def _normalize(x_ref, pre_g_ref, h_ref):
    h_ref[...] = _rms_norm(x_ref[0], pre_g_ref[...]).astype(BF16)


def _projection_steps(h_ref, w_in_ref, p_ref):
    def make(lo, width):
        def run():
            p_ref[:, lo:lo + width] = _dot(h_ref[...], w_in_ref[:, lo:lo + width])
        return run
    total = p_ref.shape[1]
    return [make(lo, min(PROJ_PIECE, total - lo)) for lo in range(0, total, PROJ_PIECE)]


def _fence(h_ref, never_ref, value):
    rows, cols = value.shape
    folded = value.astype(F32).reshape(rows // FENCE_ROWS, FENCE_ROWS, cols).sum(axis=0)
    folded = sum(folded[:, c:c + LANES] for c in range(0, cols, LANES))
    keep = h_ref[0:FENCE_ROWS, 0:LANES]
    h_ref[0:FENCE_ROWS, 0:LANES] = jnp.where(never_ref[0] != 0, folded.astype(BF16), keep)


class _Filler:
    def __init__(self, steps, h_ref, never_ref):
        self._steps = list(steps)
        self._h_ref = h_ref
        self._never_ref = never_ref

    def __call__(self, count=1, after=None):
        if after is not None and self._steps:
            _fence(self._h_ref, self._never_ref, after)
        for _ in range(count):
            if self._steps:
                self._steps.pop(0)()

    def drain(self):
        self(len(self._steps))


def _gated(y, gate):
    return y * jax.nn.silu(gate)


def _finish(x_ref, y_ref, w_out_ref, post_g_ref, o_ref):
    o_ref[0] = x_ref[0] + _rms_norm(_dot(y_ref[...], w_out_ref[...]), post_g_ref[...])


def _prime_pipeline(n, h_refs, p_refs):
    @pl.when(n == 0)
    def _():
        h_refs[1][...] = jnp.zeros(h_refs[1].shape, BF16)
        p_refs[0][...] = jnp.zeros(p_refs[0].shape, F32)


def _even_mix(p_ref, x_ref, kt_ref, v_ref, ln_g_ref, ln_b_ref, ws_ref, bs_ref, bconv_ref, w_out_ref,
              post_g_ref, o_ref, bext_ref, vn_ref, y_ref, fill):
    tile = p_ref.shape[0]
    bw = BRANCH_WIDTH
    g0 = 5 * bw + XA_WIDTH

    def seg(lo, width):
        return p_ref[:, lo:lo + width]

    fill()

    vn = _layer_norm(seg(bw, bw), ln_g_ref[...], ln_b_ref[...])
    vn_ref[...] = vn.astype(BF16)
    fill(2)

    cx = seg(3 * bw, bw) * seg(4 * bw, bw)
    for s in range(N_SLABS):
        bext_ref[s, B_HALO:B_HALO + tile, :] = _slab(cx, s)
    conv = jnp.concatenate([_causal_dwconv_slab(bext_ref, bconv_ref, s, B_HALO, SHORT_CONV, tile)
                            for s in range(N_SLABS)], axis=-1)
    yb = _gated(seg(2 * bw, bw) * conv, seg(g0 + bw, bw))
    y_ref[:, bw:2 * bw] = yb.astype(BF16)
    _carry_halo(bext_ref, B_HALO, tile)
    fill(2)

    low_half = lax.broadcasted_iota(jnp.int32, (CHUNK, LANES), 1) < (LANES // 2)
    for c in range(tile // CHUNK):
        rows = pl.ds(c * CHUNK, CHUNK)
        r0 = _dot(ws_ref[0], vn_ref[rows, 0:256])
        r1 = _dot(ws_ref[1], vn_ref[rows, 128:384])
        r2 = _dot(ws_ref[2], vn_ref[rows, 384:640])
        r3 = _dot(ws_ref[3], vn_ref[rows, 512:768])
        sg = jnp.concatenate([
            r0[:, 0:128], jnp.where(low_half, r0[:, 128:256], r1[:, 0:128]), r1[:, 128:256],
            r2[:, 0:128], jnp.where(low_half, r2[:, 128:256], r3[:, 0:128]), r3[:, 128:256]],
            axis=-1) + bs_ref[...]
        ya = _gated(p_ref[rows, 0:bw] * sg, p_ref[rows, g0:g0 + bw])
        y_ref[rows, 0:bw] = ya.astype(BF16)
        fill(1)

    yx = _gated(_cross_attention(seg(5 * bw, XA_WIDTH), kt_ref, v_ref), seg(g0 + 2 * bw, XA_WIDTH))
    y_ref[:, 2 * bw:] = yx.astype(BF16)
    fill.drain()
    _finish(x_ref, y_ref, w_out_ref, post_g_ref, o_ref)


def _even_kernel(never_ref, xf_ref, xb_ref, kt_ref, v_ref, pre_g_ref, w_in_ref, ln_g_ref, ln_b_ref,
                 ws_ref, bs_ref, bconv_ref, w_out_ref, post_g_ref, o_ref, h0_ref, h1_ref, p0_ref, p1_ref,
                 bext_ref, vn_ref, y_ref, *, tiles_per_seq):
    n = pl.program_id(0)
    _prime_pipeline(n, (h0_ref, h1_ref), (p0_ref, p1_ref))

    @pl.when(jnp.maximum(n - PIPELINE_LAG, 0) % tiles_per_seq == 0)
    def _():
        _zero_halo(bext_ref, B_HALO)

    def step(h_new, h_old, p_new, p_old):
        fill = _Filler(_projection_steps(h_old, w_in_ref, p_new)
                       + [lambda: _normalize(xf_ref, pre_g_ref, h_new)], h_old, never_ref)
        _even_mix(p_old, xb_ref, kt_ref, v_ref, ln_g_ref, ln_b_ref, ws_ref, bs_ref, bconv_ref,
                  w_out_ref, post_g_ref, o_ref, bext_ref, vn_ref, y_ref, fill)

    pl.when(n % 2 == 0)(lambda: step(h0_ref, h1_ref, p1_ref, p0_ref))
    pl.when(n % 2 == 1)(lambda: step(h1_ref, h0_ref, p0_ref, p1_ref))


_SLAB_WINDOWS = ((2,), (2, 4), (4,), (8,), (8, 16), (16,))


def _window_sums(ext_ref, s, tile, windows):
    sums = {}
    acc = ext_ref[s, pl.ds(C_HALO, tile), :]
    for j in range(1, max(windows)):
        acc = acc + ext_ref[s, pl.ds(C_HALO - j, tile), :]
        if j + 1 in windows:
            sums[j + 1] = acc
    return sums


def _odd_mix(p_ref, x_ref, kt_ref, v_ref, inv_cnt_ref, wgrp_ref, c_scale_ref, dw_w_ref, dw_b_ref,
             ln_g_ref, ln_b_ref, pw_w_ref, pw_b_ref, w_out_ref, post_g_ref, o_ref, cext_ref, zext_ref,
             conv_ref, act_ref, y_ref, fill):
    tile = p_ref.shape[0]
    bw = BRANCH_WIDTH
    g0 = 3 * bw + XA_WIDTH

    def seg(lo, width):
        return p_ref[:, lo:lo + width]

    fill()

    z = seg(bw, bw) * jax.nn.sigmoid(seg(2 * bw, bw))
    for s in range(N_SLABS):
        zext_ref[s, D_HALO:D_HALO + tile, :] = _slab(z, s)
    for s in range(N_SLABS):
        acc = _causal_dwconv_slab(zext_ref, dw_w_ref, s, D_HALO, CONF_CONV, tile)
        conv_ref[:, s * LANES:(s + 1) * LANES] = acc
        fill(1, after=acc)
    _carry_halo(zext_ref, D_HALO, tile)
    zd = jax.nn.silu(_layer_norm(conv_ref[...] + dw_b_ref[...], ln_g_ref[...], ln_b_ref[...]))
    act_ref[...] = zd.astype(BF16)
    fill(1)
    yd = _gated(_dot(act_ref[...], pw_w_ref[...]) + pw_b_ref[...], seg(g0 + bw, bw))
    y_ref[:, bw:2 * bw] = yd.astype(BF16)

    zc = seg(0, bw)
    for s in range(N_SLABS):
        cext_ref[s, C_HALO:C_HALO + tile, :] = _slab(zc, s)
    low_half = lax.broadcasted_iota(jnp.int32, (tile, LANES), 1) < (LANES // 2)
    pooled = []
    for s in range(N_SLABS):
        sums = _window_sums(cext_ref, s, tile, _SLAB_WINDOWS[s])
        wins = _SLAB_WINDOWS[s]
        tot = sums[wins[0]] if len(wins) == 1 else jnp.where(low_half, sums[wins[0]], sums[wins[1]])
        pooled.append(tot * inv_cnt_ref[0, :, s * LANES:(s + 1) * LANES] - _slab(zc, s))
    pooled = jnp.concatenate(pooled, axis=-1)
    act_ref[...] = pooled.astype(BF16)
    _carry_halo(cext_ref, C_HALO, tile)
    fill(1)
    half = bw // 2
    yc = jnp.concatenate([_dot(act_ref[:, :half], wgrp_ref[0]), _dot(act_ref[:, half:], wgrp_ref[1])],
                         axis=-1) * c_scale_ref[...]
    y_ref[:, 0:bw] = _gated(yc, seg(g0, bw)).astype(BF16)

    yx = _gated(_cross_attention(seg(3 * bw, XA_WIDTH), kt_ref, v_ref), seg(g0 + 2 * bw, XA_WIDTH))
    y_ref[:, 2 * bw:] = yx.astype(BF16)
    fill.drain()
    _finish(x_ref, y_ref, w_out_ref, post_g_ref, o_ref)


def _odd_kernel(never_ref, xf_ref, xb_ref, kt_ref, v_ref, inv_cnt_ref, pre_g_ref, w_in_ref, wgrp_ref,
                c_scale_ref, dw_w_ref, dw_b_ref, ln_g_ref, ln_b_ref, pw_w_ref, pw_b_ref, w_out_ref,
                post_g_ref, o_ref, h0_ref, h1_ref, p0_ref, p1_ref, cext_ref, zext_ref, conv_ref, act_ref,
                y_ref, *, tiles_per_seq):
    n = pl.program_id(0)
    _prime_pipeline(n, (h0_ref, h1_ref), (p0_ref, p1_ref))

    @pl.when(jnp.maximum(n - PIPELINE_LAG, 0) % tiles_per_seq == 0)
    def _():
        _zero_halo(cext_ref, C_HALO)
        _zero_halo(zext_ref, D_HALO)

    def step(h_new, h_old, p_new, p_old):
        fill = _Filler(_projection_steps(h_old, w_in_ref, p_new)
                       + [lambda: _normalize(xf_ref, pre_g_ref, h_new)], h_old, never_ref)
        _odd_mix(p_old, xb_ref, kt_ref, v_ref, inv_cnt_ref, wgrp_ref, c_scale_ref, dw_w_ref, dw_b_ref,
                 ln_g_ref, ln_b_ref, pw_w_ref, pw_b_ref, w_out_ref, post_g_ref, o_ref, cext_ref, zext_ref,
                 conv_ref, act_ref, y_ref, fill)

    pl.when(n % 2 == 0)(lambda: step(h0_ref, h1_ref, p1_ref, p0_ref))
    pl.when(n % 2 == 1)(lambda: step(h1_ref, h0_ref, p0_ref, p1_ref))


def _kv_kernel(mem_ref, g_ref, wkt_ref, wv_ref, kt_ref, v_ref):
    mem_n = _rms_norm(mem_ref[0], g_ref[...]).astype(BF16)
    kt = lax.dot_general(wkt_ref[...], mem_n, (((1,), (1,)), ((), ())), preferred_element_type=F32)
    kt_ref[0] = kt.astype(BF16)
    v_ref[0] = _dot(mem_n, wv_ref[...]).astype(BF16)


def _memory_kv(mem, mem_g, w_kv):
    bsz = mem.shape[0]
    wkt = w_kv[:, :XA_WIDTH].T.astype(BF16)
    wv = w_kv[:, XA_WIDTH:].astype(BF16)
    const = lambda b: (0, 0)
    return pl.pallas_call(
        _kv_kernel,
        grid=(bsz,),
        in_specs=[pl.BlockSpec((1, N_MEM, D_MODEL), lambda b: (b, 0, 0)),
                  pl.BlockSpec((1, D_MODEL), const),
                  pl.BlockSpec((XA_WIDTH, D_MODEL), const),
                  pl.BlockSpec((D_MODEL, XA_WIDTH), const)],
        out_specs=[pl.BlockSpec((1, XA_WIDTH, N_MEM), lambda b: (b, 0, 0)),
                   pl.BlockSpec((1, N_MEM, XA_WIDTH), lambda b: (b, 0, 0))],
        out_shape=[jax.ShapeDtypeStruct((bsz, XA_WIDTH, N_MEM), BF16),
                   jax.ShapeDtypeStruct((bsz, N_MEM, XA_WIDTH), BF16)],
        compiler_params=pltpu.CompilerParams(dimension_semantics=("arbitrary",)),
        name="memory_kv",
    )(mem, mem_g.reshape(1, D_MODEL), wkt, wv)


def _resident(shape):
    zeros = (0,) * len(shape)
    return pl.BlockSpec(shape, lambda n: zeros, pipeline_mode=pl.Buffered(1))


def _row(p):
    return p.reshape(1, -1)


def _layer_call(body, name, x, kt, v, in_width, extra_specs, extra_args, resident_args, scratch_shapes):
    bsz, seq, d = x.shape
    tile = SEQ_TILE
    assert seq % tile == 0 and tile % CHUNK == 0
    tiles_per_seq = seq // tile
    n_tiles = bsz * tiles_per_seq

    def front(n):
        t = jnp.minimum(n, n_tiles - 1)
        return (t // tiles_per_seq, t % tiles_per_seq, 0)

    def back(n):
        t = jnp.maximum(n - PIPELINE_LAG, 0)
        return (t // tiles_per_seq, t % tiles_per_seq, 0)

    def back_batch(n):
        return (jnp.maximum(n - PIPELINE_LAG, 0) // tiles_per_seq, 0, 0)

    in_specs = ([pl.BlockSpec(memory_space=pltpu.SMEM),
                 pl.BlockSpec((1, tile, d), front),
                 pl.BlockSpec((1, tile, d), back),
                 pl.BlockSpec((1, XA_WIDTH, N_MEM), back_batch),
                 pl.BlockSpec((1, N_MEM, XA_WIDTH), back_batch)]
                + extra_specs(tiles_per_seq) + [_resident(a.shape) for a in resident_args])
    h_buffer = pltpu.VMEM((tile, d), BF16)
    p_buffer = pltpu.VMEM((tile, in_width), F32)
    never = jnp.zeros((1,), jnp.int32)
    return pl.pallas_call(
        functools.partial(body, tiles_per_seq=tiles_per_seq),
        grid=(n_tiles + PIPELINE_LAG,),
        in_specs=in_specs,
        out_specs=pl.BlockSpec((1, tile, d), back),
        out_shape=jax.ShapeDtypeStruct(x.shape, x.dtype),
        scratch_shapes=[h_buffer, h_buffer, p_buffer, p_buffer] + scratch_shapes,
        compiler_params=pltpu.CompilerParams(dimension_semantics=("arbitrary",),
                                             vmem_limit_bytes=VMEM_LIMIT_BYTES),
        name=name,
    )(never, x, x, kt, v, *extra_args, *resident_args)


def _even_layer(x, mem, pre_g, w_in, a_ln_g, a_ln_b, a_ws, a_bs, b_conv, mem_g, w_kv, w_out, post_g):
    kt, v = _memory_kv(mem, mem_g, w_kv)
    causal = jnp.tril(jnp.ones((CHUNK, CHUNK), dtype=bool))
    ws = jnp.where(causal[None], a_ws, 0.0).astype(BF16)
    bs = jnp.repeat(a_bs.T, A_HEAD_DIM, axis=1)
    resident = [_row(pre_g), w_in.astype(BF16), _row(a_ln_g), _row(a_ln_b), ws, bs, b_conv,
                w_out.astype(BF16), _row(post_g)]
    scratch = [pltpu.VMEM((N_SLABS, B_HALO + SEQ_TILE, LANES), F32),
               pltpu.VMEM((SEQ_TILE, BRANCH_WIDTH), BF16),
               pltpu.VMEM((SEQ_TILE, MIX_WIDTH), BF16)]
    return _layer_call(_even_kernel, "even_layer", x, kt, v, EVEN_IN, lambda tps: [], [], resident, scratch)


def _odd_layer(x, mem, pre_g, w_in, c_wgrp, c_scale, d_dw_w, d_dw_b, d_ln_g, d_ln_b, d_pw_w, d_pw_b,
               mem_g, w_kv, w_out, post_g):
    kt, v = _memory_kv(mem, mem_g, w_kv)
    tile = SEQ_TILE
    win = jnp.repeat(jnp.asarray(POOL_WINDOWS, F32), C_GROUP)
    pos = jnp.arange(1, tile + 1, dtype=F32)[:, None]
    inv_cnt = jnp.stack([1.0 / jnp.minimum(pos, win[None, :]),
                         jnp.broadcast_to(1.0 / win[None, :], (tile, BRANCH_WIDTH))])
    zero = jnp.zeros((C_GROUP, C_GROUP), c_wgrp.dtype)
    wgrp = jnp.stack([jnp.block([[c_wgrp[0], zero], [zero, c_wgrp[1]]]),
                      jnp.block([[c_wgrp[2], zero], [zero, c_wgrp[3]]])]).astype(BF16)

    def extra_specs(tiles_per_seq):
        def which(n):
            return (jnp.minimum(jnp.maximum(n - PIPELINE_LAG, 0) % tiles_per_seq, 1), 0, 0)
        return [pl.BlockSpec((1, tile, BRANCH_WIDTH), which)]

    resident = [_row(pre_g), w_in.astype(BF16), wgrp, _row(c_scale), d_dw_w, _row(d_dw_b),
                _row(d_ln_g), _row(d_ln_b), d_pw_w.astype(BF16), _row(d_pw_b), w_out.astype(BF16),
                _row(post_g)]
    scratch = [pltpu.VMEM((N_SLABS, C_HALO + tile, LANES), F32),
               pltpu.VMEM((N_SLABS, D_HALO + tile, LANES), F32),
               pltpu.VMEM((tile, BRANCH_WIDTH), F32),
               pltpu.VMEM((tile, BRANCH_WIDTH), BF16),
               pltpu.VMEM((tile, MIX_WIDTH), BF16)]
    return _layer_call(_odd_kernel, "odd_layer", x, kt, v, ODD_IN, extra_specs, [inv_cnt], resident, scratch)


def kernel(x, mem, even_pre_g, even_w_in, even_a_ln_g, even_a_ln_b, even_a_ws, even_a_bs, even_b_conv, even_mem_g, even_w_kv, even_w_out, even_post_g, odd_pre_g, odd_w_in, odd_c_wgrp, odd_c_scale, odd_d_dw_w, odd_d_dw_b, odd_d_ln_g, odd_d_ln_b, odd_d_pw_w, odd_d_pw_b, odd_mem_g, odd_w_kv, odd_w_out, odd_post_g):
    depth = even_pre_g.shape[0] + odd_pre_g.shape[0]
    for layer in range(depth):
        i = layer // 2
        if layer % 2 == 0:
            x = _even_layer(x, mem, even_pre_g[i], even_w_in[i], even_a_ln_g[i], even_a_ln_b[i],
                            even_a_ws[i], even_a_bs[i], even_b_conv[i], even_mem_g[i], even_w_kv[i],
                            even_w_out[i], even_post_g[i])
        else:
            x = _odd_layer(x, mem, odd_pre_g[i], odd_w_in[i], odd_c_wgrp[i], odd_c_scale[i],
                           odd_d_dw_w[i], odd_d_dw_b[i], odd_d_ln_g[i], odd_d_ln_b[i], odd_d_pw_w[i],
                           odd_d_pw_b[i], odd_mem_g[i], odd_w_kv[i], odd_w_out[i], odd_post_g[i])
    return x
```

```python
import functools

import jax
import jax.numpy as jnp
from jax import lax
from jax.experimental import pallas as pl
from jax.experimental.pallas import tpu as pltpu

D_MODEL = 1024
N_MEM = 256
MIX_WIDTH = 2 * D_MODEL
XA_HEADS = 4
XA_WIDTH = MIX_WIDTH // 4
XA_HEAD_DIM = XA_WIDTH // XA_HEADS
BRANCH_WIDTH = (MIX_WIDTH - XA_WIDTH) // 2
CHUNK = 128
A_HEADS = 4
A_HEAD_DIM = BRANCH_WIDTH // A_HEADS
SHORT_CONV = 3
POOL_WINDOWS = (2, 4, 8, 16)
C_GROUP = BRANCH_WIDTH // len(POOL_WINDOWS)
CONF_CONV = 31
EPS = 1e-6

LANES = 128
SUBLANES = 8
N_SLABS = BRANCH_WIDTH // LANES
SEQ_TILE = 512
VMEM_LIMIT_BYTES = 56 * 1024 * 1024

B_HALO = SUBLANES
C_HALO = 2 * SUBLANES
D_HALO = 4 * SUBLANES

BF16 = jnp.bfloat16
F32 = jnp.float32


def _dot(a, b):
    return jnp.dot(a, b, preferred_element_type=F32)


def _rms_norm(x, g):
    y = x * lax.rsqrt(jnp.mean(x * x, axis=-1, keepdims=True) + EPS)
    return y * g


def _layer_norm(x, g, b):
    mu = jnp.mean(x, axis=-1, keepdims=True)
    xc = x - mu
    var = jnp.mean(xc * xc, axis=-1, keepdims=True)
    return xc * lax.rsqrt(var + EPS) * g + b


def _slab(v, s):
    return v[:, s * LANES:(s + 1) * LANES]


def _cross_attention(q, kt_ref, v_ref):
    qb = q.astype(BF16)
    outs = []
    for hd in range(XA_HEADS):
        lo, hi = hd * XA_HEAD_DIM, (hd + 1) * XA_HEAD_DIM
        s = _dot(qb[:, lo:hi], kt_ref[0, lo:hi, :]) * (XA_HEAD_DIM ** -0.5)
        m = jnp.max(s, axis=-1, keepdims=True)
        e = jnp.exp(s - m)
        p = e / jnp.sum(e, axis=-1, keepdims=True)
        outs.append(_dot(p.astype(BF16), v_ref[0, :, lo:hi]))
    return jnp.concatenate(outs, axis=-1)


def _carry_halo(ext_ref, halo, tile):
    for s in range(N_SLABS):
        ext_ref[s, 0:halo, :] = ext_ref[s, tile:tile + halo, :]


def _zero_halo(ext_ref, halo):
    ext_ref[:, 0:halo, :] = jnp.zeros((N_SLABS, halo, LANES), F32)


def _even_kernel(x_ref, kt_ref, v_ref, pre_g_ref, w_in_ref, ln_g_ref, ln_b_ref, ws_ref, bs_ref,
                 bconv_ref, w_out_ref, post_g_ref, o_ref, bext_ref):
    tile = x_ref.shape[1]
    bw = BRANCH_WIDTH

    @pl.when(pl.program_id(1) == 0)
    def _():
        _zero_halo(bext_ref, B_HALO)

    x = x_ref[0]
    h = _rms_norm(x, pre_g_ref[...]).astype(BF16)

    def proj(lo, width):
        return _dot(h, w_in_ref[:, lo:lo + width])

    u = proj(0, bw)
    v = proj(bw, bw)
    vn = _layer_norm(v, ln_g_ref[...], ln_b_ref[...]).astype(BF16)
    low_half = lax.broadcasted_iota(jnp.int32, (CHUNK, LANES), 1) < (LANES // 2)
    sg_chunks = []
    for c in range(tile // CHUNK):
        rows = vn[c * CHUNK:(c + 1) * CHUNK, :]
        r0 = _dot(ws_ref[0], rows[:, 0:256])
        r1 = _dot(ws_ref[1], rows[:, 128:384])
        r2 = _dot(ws_ref[2], rows[:, 384:640])
        r3 = _dot(ws_ref[3], rows[:, 512:768])
        sg_chunks.append(jnp.concatenate([
            r0[:, 0:128], jnp.where(low_half, r0[:, 128:256], r1[:, 0:128]), r1[:, 128:256],
            r2[:, 0:128], jnp.where(low_half, r2[:, 128:256], r3[:, 0:128]), r3[:, 128:256]],
            axis=-1) + bs_ref[...])
    ya = u * jnp.concatenate(sg_chunks, axis=0)

    bg = proj(2 * bw, bw)
    cx = proj(3 * bw, bw) * proj(4 * bw, bw)
    for s in range(N_SLABS):
        bext_ref[s, B_HALO:B_HALO + tile, :] = _slab(cx, s)
    conv = []
    for s in range(N_SLABS):
        acc = None
        for k in range(SHORT_CONV):
            term = bext_ref[s, pl.ds(B_HALO - (SHORT_CONV - 1) + k, tile), :] * bconv_ref[k:k + 1, s * LANES:(s + 1) * LANES]
            acc = term if acc is None else acc + term
        conv.append(acc)
    yb = bg * jnp.concatenate(conv, axis=-1)
    _carry_halo(bext_ref, B_HALO, tile)

    yx = _cross_attention(proj(5 * bw, XA_WIDTH), kt_ref, v_ref)

    g0 = 5 * bw + XA_WIDTH
    out = (_dot((ya * jax.nn.silu(proj(g0, bw))).astype(BF16), w_out_ref[0:bw, :])
           + _dot((yb * jax.nn.silu(proj(g0 + bw, bw))).astype(BF16), w_out_ref[bw:2 * bw, :])
           + _dot((yx * jax.nn.silu(proj(g0 + 2 * bw, XA_WIDTH))).astype(BF16), w_out_ref[2 * bw:, :]))
    o_ref[0] = x + _rms_norm(out, post_g_ref[...])


def _window_sums(ext_ref, s, tile, windows):
    sums = {}
    acc = ext_ref[s, pl.ds(C_HALO, tile), :]
    for j in range(1, max(windows)):
        acc = acc + ext_ref[s, pl.ds(C_HALO - j, tile), :]
        if j + 1 in windows:
            sums[j + 1] = acc
    return sums


_SLAB_WINDOWS = ((2,), (2, 4), (4,), (8,), (8, 16), (16,))


def _odd_kernel(x_ref, kt_ref, v_ref, inv_cnt_ref, pre_g_ref, w_in_ref, wgrp_ref, c_scale_ref,
                dw_w_ref, dw_b_ref, ln_g_ref, ln_b_ref, pw_w_ref, pw_b_ref, w_out_ref, post_g_ref,
                o_ref, cext_ref, zext_ref):
    tile = x_ref.shape[1]
    bw = BRANCH_WIDTH

    @pl.when(pl.program_id(1) == 0)
    def _():
        _zero_halo(cext_ref, C_HALO)
        _zero_halo(zext_ref, D_HALO)

    x = x_ref[0]
    h = _rms_norm(x, pre_g_ref[...]).astype(BF16)

    def proj(lo, width):
        return _dot(h, w_in_ref[:, lo:lo + width])

    zc = proj(0, bw)
    for s in range(N_SLABS):
        cext_ref[s, C_HALO:C_HALO + tile, :] = _slab(zc, s)
    low_half = lax.broadcasted_iota(jnp.int32, (tile, LANES), 1) < (LANES // 2)
    pooled = []
    for s in range(N_SLABS):
        sums = _window_sums(cext_ref, s, tile, _SLAB_WINDOWS[s])
        wins = _SLAB_WINDOWS[s]
        tot = sums[wins[0]] if len(wins) == 1 else jnp.where(low_half, sums[wins[0]], sums[wins[1]])
        pooled.append(tot * inv_cnt_ref[0, :, s * LANES:(s + 1) * LANES] - _slab(zc, s))
    _carry_halo(cext_ref, C_HALO, tile)
    pooled = jnp.concatenate(pooled, axis=-1).astype(BF16)
    half = bw // 2
    yc = jnp.concatenate([_dot(pooled[:, :half], wgrp_ref[0]), _dot(pooled[:, half:], wgrp_ref[1])],
                         axis=-1) * c_scale_ref[...]

    z = proj(bw, bw) * jax.nn.sigmoid(proj(2 * bw, bw))
    for s in range(N_SLABS):
        zext_ref[s, D_HALO:D_HALO + tile, :] = _slab(z, s)
    conv = []
    for s in range(N_SLABS):
        acc = None
        for k in range(CONF_CONV):
            term = zext_ref[s, pl.ds(D_HALO - (CONF_CONV - 1) + k, tile), :] * dw_w_ref[k:k + 1, s * LANES:(s + 1) * LANES]
            acc = term if acc is None else acc + term
        conv.append(acc)
    _carry_halo(zext_ref, D_HALO, tile)
    zd = jnp.concatenate(conv, axis=-1) + dw_b_ref[...]
    zd = jax.nn.silu(_layer_norm(zd, ln_g_ref[...], ln_b_ref[...]))
    yd = _dot(zd.astype(BF16), pw_w_ref[...]) + pw_b_ref[...]

    yx = _cross_attention(proj(3 * bw, XA_WIDTH), kt_ref, v_ref)

    g0 = 3 * bw + XA_WIDTH
    out = (_dot((yc * jax.nn.silu(proj(g0, bw))).astype(BF16), w_out_ref[0:bw, :])
           + _dot((yd * jax.nn.silu(proj(g0 + bw, bw))).astype(BF16), w_out_ref[bw:2 * bw, :])
           + _dot((yx * jax.nn.silu(proj(g0 + 2 * bw, XA_WIDTH))).astype(BF16), w_out_ref[2 * bw:, :]))
    o_ref[0] = x + _rms_norm(out, post_g_ref[...])


def _kv_kernel(mem_ref, g_ref, wkt_ref, wv_ref, kt_ref, v_ref):
    mem_n = _rms_norm(mem_ref[0], g_ref[...]).astype(BF16)
    kt = lax.dot_general(wkt_ref[...], mem_n, (((1,), (1,)), ((), ())), preferred_element_type=F32)
    kt_ref[0] = kt.astype(BF16)
    v_ref[0] = _dot(mem_n, wv_ref[...]).astype(BF16)


def _memory_kv(mem, mem_g, w_kv):
    bsz = mem.shape[0]
    wkt = w_kv[:, :XA_WIDTH].T.astype(BF16)
    wv = w_kv[:, XA_WIDTH:].astype(BF16)
    const = lambda b: (0, 0)
    return pl.pallas_call(
        _kv_kernel,
        grid=(bsz,),
        in_specs=[pl.BlockSpec((1, N_MEM, D_MODEL), lambda b: (b, 0, 0)),
                  pl.BlockSpec((1, D_MODEL), const),
                  pl.BlockSpec((XA_WIDTH, D_MODEL), const),
                  pl.BlockSpec((D_MODEL, XA_WIDTH), const)],
        out_specs=[pl.BlockSpec((1, XA_WIDTH, N_MEM), lambda b: (b, 0, 0)),
                   pl.BlockSpec((1, N_MEM, XA_WIDTH), lambda b: (b, 0, 0))],
        out_shape=[jax.ShapeDtypeStruct((bsz, XA_WIDTH, N_MEM), BF16),
                   jax.ShapeDtypeStruct((bsz, N_MEM, XA_WIDTH), BF16)],
        compiler_params=pltpu.CompilerParams(dimension_semantics=("arbitrary",)),
        name="memory_kv",
    )(mem, mem_g.reshape(1, D_MODEL), wkt, wv)


def _resident(shape):
    zeros = (0,) * len(shape)
    return pl.BlockSpec(shape, lambda b, j: zeros, pipeline_mode=pl.Buffered(1))


def _row(p):
    return p.reshape(1, -1)


def _layer_call(body, name, x, kt, v, extra_specs, extra_args, resident_args, scratch_shapes):
    bsz, seq, d = x.shape
    tile = SEQ_TILE
    assert seq % tile == 0 and tile % CHUNK == 0
    tile_spec = pl.BlockSpec((1, tile, d), lambda b, j: (b, j, 0))
    in_specs = ([tile_spec,
                 pl.BlockSpec((1, XA_WIDTH, N_MEM), lambda b, j: (b, 0, 0)),
                 pl.BlockSpec((1, N_MEM, XA_WIDTH), lambda b, j: (b, 0, 0))]
                + extra_specs + [_resident(a.shape) for a in resident_args])
    return pl.pallas_call(
        body,
        grid=(bsz, seq // tile),
        in_specs=in_specs,
        out_specs=tile_spec,
        out_shape=jax.ShapeDtypeStruct(x.shape, x.dtype),
        scratch_shapes=scratch_shapes,
        compiler_params=pltpu.CompilerParams(dimension_semantics=("arbitrary", "arbitrary"),
                                             vmem_limit_bytes=VMEM_LIMIT_BYTES),
        name=name,
    )(x, kt, v, *extra_args, *resident_args)


def _even_layer(x, mem, pre_g, w_in, a_ln_g, a_ln_b, a_ws, a_bs, b_conv, mem_g, w_kv, w_out, post_g):
    kt, v = _memory_kv(mem, mem_g, w_kv)
    causal = jnp.tril(jnp.ones((CHUNK, CHUNK), dtype=bool))
    ws = jnp.where(causal[None], a_ws, 0.0).astype(BF16)
    bs = jnp.repeat(a_bs.T, A_HEAD_DIM, axis=1)
    resident = [_row(pre_g), w_in.astype(BF16), _row(a_ln_g), _row(a_ln_b), ws, bs, b_conv,
                w_out.astype(BF16), _row(post_g)]
    scratch = [pltpu.VMEM((N_SLABS, B_HALO + SEQ_TILE, LANES), F32)]
    return _layer_call(_even_kernel, "even_layer", x, kt, v, [], [], resident, scratch)


def _odd_layer(x, mem, pre_g, w_in, c_wgrp, c_scale, d_dw_w, d_dw_b, d_ln_g, d_ln_b, d_pw_w, d_pw_b,
               mem_g, w_kv, w_out, post_g):
    kt, v = _memory_kv(mem, mem_g, w_kv)
    tile = SEQ_TILE
    win = jnp.repeat(jnp.asarray(POOL_WINDOWS, F32), C_GROUP)
    pos = jnp.arange(1, tile + 1, dtype=F32)[:, None]
    inv_cnt = jnp.stack([1.0 / jnp.minimum(pos, win[None, :]),
                         jnp.broadcast_to(1.0 / win[None, :], (tile, BRANCH_WIDTH))])
    zero = jnp.zeros((C_GROUP, C_GROUP), c_wgrp.dtype)
    wgrp = jnp.stack([jnp.block([[c_wgrp[0], zero], [zero, c_wgrp[1]]]),
                      jnp.block([[c_wgrp[2], zero], [zero, c_wgrp[3]]])]).astype(BF16)
    extra_specs = [pl.BlockSpec((1, tile, BRANCH_WIDTH), lambda b, j: (jnp.minimum(j, 1), 0, 0))]
    resident = [_row(pre_g), w_in.astype(BF16), wgrp, _row(c_scale), d_dw_w, _row(d_dw_b),
                _row(d_ln_g), _row(d_ln_b), d_pw_w.astype(BF16), _row(d_pw_b), w_out.astype(BF16),
                _row(post_g)]
    scratch = [pltpu.VMEM((N_SLABS, C_HALO + tile, LANES), F32),
               pltpu.VMEM((N_SLABS, D_HALO + tile, LANES), F32)]
    return _layer_call(_odd_kernel, "odd_layer", x, kt, v, extra_specs, [inv_cnt], resident, scratch)


def kernel(x, mem, even_pre_g, even_w_in, even_a_ln_g, even_a_ln_b, even_a_ws, even_a_bs, even_b_conv, even_mem_g, even_w_kv, even_w_out, even_post_g, odd_pre_g, odd_w_in, odd_c_wgrp, odd_c_scale, odd_d_dw_w, odd_d_dw_b, odd_d_ln_g, odd_d_ln_b, odd_d_pw_w, odd_d_pw_b, odd_mem_g, odd_w_kv, odd_w_out, odd_post_g):
    depth = even_pre_g.shape[0] + odd_pre_g.shape[0]
    for layer in range(depth):
        i = layer // 2
        if layer % 2 == 0:
            x = _even_layer(x, mem, even_pre_g[i], even_w_in[i], even_a_ln_g[i], even_a_ln_b[i],
                            even_a_ws[i], even_a_bs[i], even_b_conv[i], even_mem_g[i], even_w_kv[i],
                            even_w_out[i], even_post_g[i])
        else:
            x = _odd_layer(x, mem, odd_pre_g[i], odd_w_in[i], odd_c_wgrp[i], odd_c_scale[i],
                           odd_d_dw_w[i], odd_d_dw_b[i], odd_d_ln_g[i], odd_d_ln_b[i], odd_d_pw_w[i],
                           odd_d_pw_b[i], odd_mem_g[i], odd_w_kv[i], odd_w_out[i], odd_post_g[i])
    return x
```

```python
import functools

import jax
import jax.numpy as jnp
from jax import lax
from jax.experimental import pallas as pl
from jax.experimental.pallas import tpu as pltpu

D_MODEL = 1024
N_MEM = 256
MIX_WIDTH = 2 * D_MODEL
XA_HEADS = 4
XA_WIDTH = MIX_WIDTH // 4
XA_HEAD_DIM = XA_WIDTH // XA_HEADS
BRANCH_WIDTH = (MIX_WIDTH - XA_WIDTH) // 2
CHUNK = 128
A_HEADS = 4
A_HEAD_DIM = BRANCH_WIDTH // A_HEADS
SHORT_CONV = 3
POOL_WINDOWS = (2, 4, 8, 16)
C_GROUP = BRANCH_WIDTH // len(POOL_WINDOWS)
CONF_CONV = 31
EPS = 1e-6

LANES = 128
SUBLANES = 8
N_SLABS = BRANCH_WIDTH // LANES
EVEN_TILE = 1024
ODD_TILE = 512
FENCE_ROWS = 2 * SUBLANES
VMEM_LIMIT_BYTES = 56 * 1024 * 1024

B_HALO = SUBLANES
C_HALO = 2 * SUBLANES
D_HALO = 4 * SUBLANES

BF16 = jnp.bfloat16
F32 = jnp.float32


def _dot(a, b):
    return jnp.dot(a, b, preferred_element_type=F32)


def _rms_norm(x, g):
    y = x * lax.rsqrt(jnp.mean(x * x, axis=-1, keepdims=True) + EPS)
    return y * g


def _layer_norm(x, g, b):
    mu = jnp.mean(x, axis=-1, keepdims=True)
    xc = x - mu
    var = jnp.mean(xc * xc, axis=-1, keepdims=True)
    return xc * lax.rsqrt(var + EPS) * g + b


def _slab(v, s):
    return v[:, s * LANES:(s + 1) * LANES]


def _cross_attention(q, kt_ref, v_ref):
    qb = q.astype(BF16)
    outs = []
    for hd in range(XA_HEADS):
        lo, hi = hd * XA_HEAD_DIM, (hd + 1) * XA_HEAD_DIM
        s = _dot(qb[:, lo:hi], kt_ref[0, lo:hi, :]) * (XA_HEAD_DIM ** -0.5)
        m = jnp.max(s, axis=-1, keepdims=True)
        e = jnp.exp(s - m)
        p = e / jnp.sum(e, axis=-1, keepdims=True)
        outs.append(_dot(p.astype(BF16), v_ref[0, :, lo:hi]))
    return jnp.concatenate(outs, axis=-1)


def _after(h, never_ref, value):
    rows, cols = value.shape
    folded = value.reshape(rows // FENCE_ROWS, FENCE_ROWS, cols).sum(axis=0)
    folded = sum(folded[:, c:c + LANES] for c in range(0, cols, LANES))
    first = jnp.where(never_ref[0] != 0, folded.astype(BF16), h[0:FENCE_ROWS, 0:LANES])
    top = jnp.concatenate([first, h[0:FENCE_ROWS, LANES:]], axis=1)
    return jnp.concatenate([top, h[FENCE_ROWS:]], axis=0)


def _carry_halo(ext_ref, halo, tile):
    for s in range(N_SLABS):
        ext_ref[s, 0:halo, :] = ext_ref[s, tile:tile + halo, :]


def _zero_halo(ext_ref, halo):
    ext_ref[:, 0:halo, :] = jnp.zeros((N_SLABS, halo, LANES), F32)


def _even_kernel(x_ref, kt_ref, v_ref, pre_g_ref, w_in_ref, ln_g_ref, ln_b_ref, ws_ref, bs_ref,
                 bconv_ref, w_out_ref, post_g_ref, o_ref, bext_ref):
    tile = x_ref.shape[1]
    bw = BRANCH_WIDTH

    @pl.when(pl.program_id(1) == 0)
    def _():
        _zero_halo(bext_ref, B_HALO)

    x = x_ref[0]
    h = _rms_norm(x, pre_g_ref[...]).astype(BF16)

    def proj(lo, width):
        return _dot(h, w_in_ref[:, lo:lo + width])

    u = proj(0, bw)
    v = proj(bw, bw)
    vn = _layer_norm(v, ln_g_ref[...], ln_b_ref[...]).astype(BF16)
    low_half = lax.broadcasted_iota(jnp.int32, (CHUNK, LANES), 1) < (LANES // 2)
    sg_chunks = []
    for c in range(tile // CHUNK):
        rows = vn[c * CHUNK:(c + 1) * CHUNK, :]
        r0 = _dot(ws_ref[0], rows[:, 0:256])
        r1 = _dot(ws_ref[1], rows[:, 128:384])
        r2 = _dot(ws_ref[2], rows[:, 384:640])
        r3 = _dot(ws_ref[3], rows[:, 512:768])
        sg_chunks.append(jnp.concatenate([
            r0[:, 0:128], jnp.where(low_half, r0[:, 128:256], r1[:, 0:128]), r1[:, 128:256],
            r2[:, 0:128], jnp.where(low_half, r2[:, 128:256], r3[:, 0:128]), r3[:, 128:256]],
            axis=-1) + bs_ref[...])
    ya = u * jnp.concatenate(sg_chunks, axis=0)

    bg = proj(2 * bw, bw)
    cx = proj(3 * bw, bw) * proj(4 * bw, bw)
    for s in range(N_SLABS):
        bext_ref[s, B_HALO:B_HALO + tile, :] = _slab(cx, s)
    conv = []
    for s in range(N_SLABS):
        acc = None
        for k in range(SHORT_CONV):
            term = bext_ref[s, pl.ds(B_HALO - (SHORT_CONV - 1) + k, tile), :] * bconv_ref[k:k + 1, s * LANES:(s + 1) * LANES]
            acc = term if acc is None else acc + term
        conv.append(acc)
    yb = bg * jnp.concatenate(conv, axis=-1)
    _carry_halo(bext_ref, B_HALO, tile)

    yx = _cross_attention(proj(5 * bw, XA_WIDTH), kt_ref, v_ref)

    g0 = 5 * bw + XA_WIDTH
    out = (_dot((ya * jax.nn.silu(proj(g0, bw))).astype(BF16), w_out_ref[0:bw, :])
           + _dot((yb * jax.nn.silu(proj(g0 + bw, bw))).astype(BF16), w_out_ref[bw:2 * bw, :])
           + _dot((yx * jax.nn.silu(proj(g0 + 2 * bw, XA_WIDTH))).astype(BF16), w_out_ref[2 * bw:, :]))
    o_ref[0] = x + _rms_norm(out, post_g_ref[...])


def _window_sums(ext_ref, s, tile, windows):
    sums = {}
    acc = ext_ref[s, pl.ds(C_HALO, tile), :]
    for j in range(1, max(windows)):
        acc = acc + ext_ref[s, pl.ds(C_HALO - j, tile), :]
        if j + 1 in windows:
            sums[j + 1] = acc
    return sums


_SLAB_WINDOWS = ((2,), (2, 4), (4,), (8,), (8, 16), (16,))


def _odd_kernel(x_ref, kt_ref, v_ref, inv_cnt_ref, never_ref, pre_g_ref, w_in_ref, wgrp_ref, c_scale_ref,
                dw_w_ref, dw_b_ref, ln_g_ref, ln_b_ref, pw_w_ref, pw_b_ref, w_out_ref, post_g_ref,
                o_ref, cext_ref, zext_ref):
    tile = x_ref.shape[1]
    bw = BRANCH_WIDTH

    @pl.when(pl.program_id(1) == 0)
    def _():
        _zero_halo(cext_ref, C_HALO)
        _zero_halo(zext_ref, D_HALO)

    x = x_ref[0]
    h = _rms_norm(x, pre_g_ref[...]).astype(BF16)
    g0 = 3 * bw + XA_WIDTH

    def proj(lo, width, lhs=None):
        return _dot(h if lhs is None else lhs, w_in_ref[:, lo:lo + width])

    z = proj(bw, bw) * jax.nn.sigmoid(proj(2 * bw, bw))
    for s in range(N_SLABS):
        zext_ref[s, D_HALO:D_HALO + tile, :] = _slab(z, s)
    later = [(0, bw), (3 * bw, XA_WIDTH), (g0, bw), (g0 + bw, bw), (g0 + 2 * bw, XA_WIDTH)]
    projected = {later[0]: proj(*later[0])}
    conv = []
    for s in range(N_SLABS):
        acc = None
        for k in range(CONF_CONV):
            term = zext_ref[s, pl.ds(D_HALO - (CONF_CONV - 1) + k, tile), :] * dw_w_ref[k:k + 1, s * LANES:(s + 1) * LANES]
            acc = term if acc is None else acc + term
        conv.append(acc)
        if s + 1 < len(later):
            projected[later[s + 1]] = proj(*later[s + 1], lhs=_after(h, never_ref, acc))
    _carry_halo(zext_ref, D_HALO, tile)

    zc = projected[(0, bw)]
    for s in range(N_SLABS):
        cext_ref[s, C_HALO:C_HALO + tile, :] = _slab(zc, s)
    low_half = lax.broadcasted_iota(jnp.int32, (tile, LANES), 1) < (LANES // 2)
    pooled = []
    for s in range(N_SLABS):
        sums = _window_sums(cext_ref, s, tile, _SLAB_WINDOWS[s])
        wins = _SLAB_WINDOWS[s]
        tot = sums[wins[0]] if len(wins) == 1 else jnp.where(low_half, sums[wins[0]], sums[wins[1]])
        inv = inv_cnt_ref[0, :, s * LANES:(s + 1) * LANES]
        mean = jnp.concatenate([tot[0:C_HALO] * inv, tot[C_HALO:] * inv[C_HALO - 1:C_HALO]], axis=0)
        pooled.append(mean - _slab(zc, s))
    _carry_halo(cext_ref, C_HALO, tile)
    pooled = jnp.concatenate(pooled, axis=-1).astype(BF16)
    half = bw // 2
    yc = jnp.concatenate([_dot(pooled[:, :half], wgrp_ref[0]), _dot(pooled[:, half:], wgrp_ref[1])],
                         axis=-1) * c_scale_ref[...]

    yx = _cross_attention(projected[(3 * bw, XA_WIDTH)], kt_ref, v_ref)

    zd = jnp.concatenate(conv, axis=-1) + dw_b_ref[...]
    zd = jax.nn.silu(_layer_norm(zd, ln_g_ref[...], ln_b_ref[...]))
    yd = _dot(zd.astype(BF16), pw_w_ref[...]) + pw_b_ref[...]

    out = (_dot((yc * jax.nn.silu(projected[(g0, bw)])).astype(BF16), w_out_ref[0:bw, :])
           + _dot((yx * jax.nn.silu(projected[(g0 + 2 * bw, XA_WIDTH)])).astype(BF16), w_out_ref[2 * bw:, :])
           + _dot((yd * jax.nn.silu(projected[(g0 + bw, bw)])).astype(BF16), w_out_ref[bw:2 * bw, :]))
    o_ref[0] = x + _rms_norm(out, post_g_ref[...])


def _kv_kernel(mem_ref, g_ref, wkt_ref, wv_ref, kt_ref, v_ref):
    mem_n = _rms_norm(mem_ref[0], g_ref[...]).astype(BF16)
    kt = lax.dot_general(wkt_ref[...], mem_n, (((1,), (1,)), ((), ())), preferred_element_type=F32)
    kt_ref[0] = kt.astype(BF16)
    v_ref[0] = _dot(mem_n, wv_ref[...]).astype(BF16)


def _memory_kv(mem, mem_g, w_kv):
    bsz = mem.shape[0]
    wkt = w_kv[:, :XA_WIDTH].T.astype(BF16)
    wv = w_kv[:, XA_WIDTH:].astype(BF16)
    const = lambda b: (0, 0)
    return pl.pallas_call(
        _kv_kernel,
        grid=(bsz,),
        in_specs=[pl.BlockSpec((1, N_MEM, D_MODEL), lambda b: (b, 0, 0)),
                  pl.BlockSpec((1, D_MODEL), const),
                  pl.BlockSpec((XA_WIDTH, D_MODEL), const),
                  pl.BlockSpec((D_MODEL, XA_WIDTH), const)],
        out_specs=[pl.BlockSpec((1, XA_WIDTH, N_MEM), lambda b: (b, 0, 0)),
                   pl.BlockSpec((1, N_MEM, XA_WIDTH), lambda b: (b, 0, 0))],
        out_shape=[jax.ShapeDtypeStruct((bsz, XA_WIDTH, N_MEM), BF16),
                   jax.ShapeDtypeStruct((bsz, N_MEM, XA_WIDTH), BF16)],
        compiler_params=pltpu.CompilerParams(dimension_semantics=("arbitrary",)),
        name="memory_kv",
    )(mem, mem_g.reshape(1, D_MODEL), wkt, wv)


def _resident(shape):
    zeros = (0,) * len(shape)
    return pl.BlockSpec(shape, lambda b, j: zeros, pipeline_mode=pl.Buffered(1))


def _row(p):
    return p.reshape(1, -1)


def _layer_call(body, name, tile, x, kt, v, extra_specs, extra_args, resident_args, scratch_shapes):
    bsz, seq, d = x.shape
    assert seq % tile == 0 and tile % CHUNK == 0
    tile_spec = pl.BlockSpec((1, tile, d), lambda b, j: (b, j, 0))
    in_specs = ([tile_spec,
                 pl.BlockSpec((1, XA_WIDTH, N_MEM), lambda b, j: (b, 0, 0)),
                 pl.BlockSpec((1, N_MEM, XA_WIDTH), lambda b, j: (b, 0, 0))]
                + extra_specs + [_resident(a.shape) for a in resident_args])
    return pl.pallas_call(
        body,
        grid=(bsz, seq // tile),
        in_specs=in_specs,
        out_specs=tile_spec,
        out_shape=jax.ShapeDtypeStruct(x.shape, x.dtype),
        scratch_shapes=scratch_shapes,
        compiler_params=pltpu.CompilerParams(dimension_semantics=("arbitrary", "arbitrary"),
                                             vmem_limit_bytes=VMEM_LIMIT_BYTES),
        name=name,
    )(x, kt, v, *extra_args, *resident_args)


def _even_layer(x, mem, pre_g, w_in, a_ln_g, a_ln_b, a_ws, a_bs, b_conv, mem_g, w_kv, w_out, post_g):
    kt, v = _memory_kv(mem, mem_g, w_kv)
    causal = jnp.tril(jnp.ones((CHUNK, CHUNK), dtype=bool))
    ws = jnp.where(causal[None], a_ws, 0.0).astype(BF16)
    bs = jnp.repeat(a_bs.T, A_HEAD_DIM, axis=1)
    resident = [_row(pre_g), w_in.astype(BF16), _row(a_ln_g), _row(a_ln_b), ws, bs, b_conv,
                w_out.astype(BF16), _row(post_g)]
    scratch = [pltpu.VMEM((N_SLABS, B_HALO + EVEN_TILE, LANES), F32)]
    return _layer_call(_even_kernel, "even_layer", EVEN_TILE, x, kt, v, [], [], resident, scratch)


def _odd_layer(x, mem, pre_g, w_in, c_wgrp, c_scale, d_dw_w, d_dw_b, d_ln_g, d_ln_b, d_pw_w, d_pw_b,
               mem_g, w_kv, w_out, post_g):
    kt, v = _memory_kv(mem, mem_g, w_kv)
    tile = ODD_TILE
    win = jnp.repeat(jnp.asarray(POOL_WINDOWS, F32), C_GROUP)
    pos = jnp.arange(1, C_HALO + 1, dtype=F32)[:, None]
    assert max(POOL_WINDOWS) <= C_HALO
    inv_cnt = jnp.stack([1.0 / jnp.minimum(pos, win[None, :]),
                         jnp.broadcast_to(1.0 / win[None, :], (C_HALO, BRANCH_WIDTH))])
    zero = jnp.zeros((C_GROUP, C_GROUP), c_wgrp.dtype)
    wgrp = jnp.stack([jnp.block([[c_wgrp[0], zero], [zero, c_wgrp[1]]]),
                      jnp.block([[c_wgrp[2], zero], [zero, c_wgrp[3]]])]).astype(BF16)
    extra_specs = [pl.BlockSpec((1, C_HALO, BRANCH_WIDTH), lambda b, j: (jnp.minimum(j, 1), 0, 0)),
                   pl.BlockSpec(memory_space=pltpu.SMEM)]
    never = jnp.zeros((1,), jnp.int32)
    resident = [_row(pre_g), w_in.astype(BF16), wgrp, _row(c_scale), d_dw_w, _row(d_dw_b),
                _row(d_ln_g), _row(d_ln_b), d_pw_w.astype(BF16), _row(d_pw_b), w_out.astype(BF16),
                _row(post_g)]
    scratch = [pltpu.VMEM((N_SLABS, C_HALO + tile, LANES), F32),
               pltpu.VMEM((N_SLABS, D_HALO + tile, LANES), F32)]
    return _layer_call(_odd_kernel, "odd_layer", tile, x, kt, v, extra_specs, [inv_cnt, never], resident, scratch)


def kernel(x, mem, even_pre_g, even_w_in, even_a_ln_g, even_a_ln_b, even_a_ws, even_a_bs, even_b_conv, even_mem_g, even_w_kv, even_w_out, even_post_g, odd_pre_g, odd_w_in, odd_c_wgrp, odd_c_scale, odd_d_dw_w, odd_d_dw_b, odd_d_ln_g, odd_d_ln_b, odd_d_pw_w, odd_d_pw_b, odd_mem_g, odd_w_kv, odd_w_out, odd_post_g):
    depth = even_pre_g.shape[0] + odd_pre_g.shape[0]
    for layer in range(depth):
        i = layer // 2
        if layer % 2 == 0:
            x = _even_layer(x, mem, even_pre_g[i], even_w_in[i], even_a_ln_g[i], even_a_ln_b[i],
                            even_a_ws[i], even_a_bs[i], even_b_conv[i], even_mem_g[i], even_w_kv[i],
                            even_w_out[i], even_post_g[i])
        else:
            x = _odd_layer(x, mem, odd_pre_g[i], odd_w_in[i], odd_c_wgrp[i], odd_c_scale[i],
                           odd_d_dw_w[i], odd_d_dw_b[i], odd_d_ln_g[i], odd_d_ln_b[i], odd_d_pw_w[i],
                           odd_d_pw_b[i], odd_mem_g[i], odd_w_kv[i], odd_w_out[i], odd_post_g[i])
    return x
```

```python
import functools

import jax
import jax.numpy as jnp
from jax import lax
from jax.experimental import pallas as pl
from jax.experimental.pallas import tpu as pltpu

D_MODEL = 1024
N_MEM = 256
MIX_WIDTH = 2 * D_MODEL
XA_HEADS = 4
XA_WIDTH = MIX_WIDTH // 4
XA_HEAD_DIM = XA_WIDTH // XA_HEADS
BRANCH_WIDTH = (MIX_WIDTH - XA_WIDTH) // 2
CHUNK = 128
A_HEADS = 4
A_HEAD_DIM = BRANCH_WIDTH // A_HEADS
SHORT_CONV = 3
POOL_WINDOWS = (2, 4, 8, 16)
C_GROUP = BRANCH_WIDTH // len(POOL_WINDOWS)
CONF_CONV = 31
EPS = 1e-6

LANES = 128
SUBLANES = 8
N_SLABS = BRANCH_WIDTH // LANES
EVEN_TILE = 1024
ODD_TILE = 512
FENCE_ROWS = 2 * SUBLANES
VMEM_LIMIT_BYTES = 56 * 1024 * 1024

B_HALO = SUBLANES
C_HALO = 2 * SUBLANES
D_HALO = 4 * SUBLANES

BF16 = jnp.bfloat16
F32 = jnp.float32


def _dot(a, b):
    return jnp.dot(a, b, preferred_element_type=F32)


def _dot_pair(a, b):
    half = a.shape[0] // 2
    return jnp.concatenate([_dot(a[:half], b), _dot(a[half:], b)], axis=0)


def _rms_norm(x, g):
    y = x * lax.rsqrt(jnp.mean(x * x, axis=-1, keepdims=True) + EPS)
    return y * g


def _layer_norm(x, g, b):
    mu = jnp.mean(x, axis=-1, keepdims=True)
    xc = x - mu
    var = jnp.mean(xc * xc, axis=-1, keepdims=True)
    return xc * lax.rsqrt(var + EPS) * g + b


def _slab(v, s):
    return v[:, s * LANES:(s + 1) * LANES]


def _cross_attention(q, kt_ref, v_ref):
    qb = q.astype(BF16)
    outs = []
    for hd in range(XA_HEADS):
        lo, hi = hd * XA_HEAD_DIM, (hd + 1) * XA_HEAD_DIM
        s = _dot(qb[:, lo:hi], kt_ref[0, lo:hi, :]) * (XA_HEAD_DIM ** -0.5)
        m = jnp.max(s, axis=-1, keepdims=True)
        e = jnp.exp(s - m)
        p = e / jnp.sum(e, axis=-1, keepdims=True)
        outs.append(_dot(p.astype(BF16), v_ref[0, :, lo:hi]))
    return jnp.concatenate(outs, axis=-1)


def _after(h, never_ref, value):
    rows, cols = value.shape
    folded = value.reshape(rows // FENCE_ROWS, FENCE_ROWS, cols).sum(axis=0)
    folded = sum(folded[:, c:c + LANES] for c in range(0, cols, LANES))
    first = jnp.where(never_ref[0] != 0, folded.astype(BF16), h[0:FENCE_ROWS, 0:LANES])
    top = jnp.concatenate([first, h[0:FENCE_ROWS, LANES:]], axis=1)
    return jnp.concatenate([top, h[FENCE_ROWS:]], axis=0)


def _carry_halo(ext_ref, halo, tile):
    for s in range(N_SLABS):
        ext_ref[s, 0:halo, :] = ext_ref[s, tile:tile + halo, :]


def _zero_halo(ext_ref, halo):
    ext_ref[:, 0:halo, :] = jnp.zeros((N_SLABS, halo, LANES), F32)


def _even_kernel(x_ref, kt_ref, v_ref, pre_g_ref, w_in_ref, ln_g_ref, ln_b_ref, ws_ref, bs_ref,
                 bconv_ref, w_out_ref, post_g_ref, o_ref, bext_ref):
    tile = x_ref.shape[1]
    bw = BRANCH_WIDTH

    @pl.when(pl.program_id(1) == 0)
    def _():
        _zero_halo(bext_ref, B_HALO)

    x = x_ref[0]
    h = _rms_norm(x, pre_g_ref[...]).astype(BF16)

    def proj(lo, width):
        return _dot(h, w_in_ref[:, lo:lo + width])

    u = proj(0, bw)
    v = proj(bw, bw)
    vn = _layer_norm(v, ln_g_ref[...], ln_b_ref[...]).astype(BF16)
    low_half = lax.broadcasted_iota(jnp.int32, (CHUNK, LANES), 1) < (LANES // 2)
    sg_chunks = []
    for c in range(tile // CHUNK):
        rows = vn[c * CHUNK:(c + 1) * CHUNK, :]
        r0 = _dot(ws_ref[0], rows[:, 0:256])
        r1 = _dot(ws_ref[1], rows[:, 128:384])
        r2 = _dot(ws_ref[2], rows[:, 384:640])
        r3 = _dot(ws_ref[3], rows[:, 512:768])
        sg_chunks.append(jnp.concatenate([
            r0[:, 0:128], jnp.where(low_half, r0[:, 128:256], r1[:, 0:128]), r1[:, 128:256],
            r2[:, 0:128], jnp.where(low_half, r2[:, 128:256], r3[:, 0:128]), r3[:, 128:256]],
            axis=-1) + bs_ref[...])
    ya = u * jnp.concatenate(sg_chunks, axis=0)

    bg = proj(2 * bw, bw)
    cx = proj(3 * bw, bw) * proj(4 * bw, bw)
    for s in range(N_SLABS):
        bext_ref[s, B_HALO:B_HALO + tile, :] = _slab(cx, s)
    conv = []
    for s in range(N_SLABS):
        acc = None
        for k in range(SHORT_CONV):
            term = bext_ref[s, pl.ds(B_HALO - (SHORT_CONV - 1) + k, tile), :] * bconv_ref[k:k + 1, s * LANES:(s + 1) * LANES]
            acc = term if acc is None else acc + term
        conv.append(acc)
    yb = bg * jnp.concatenate(conv, axis=-1)
    _carry_halo(bext_ref, B_HALO, tile)

    yx = _cross_attention(proj(5 * bw, XA_WIDTH), kt_ref, v_ref)

    g0 = 5 * bw + XA_WIDTH
    out = (_dot((ya * jax.nn.silu(proj(g0, bw))).astype(BF16), w_out_ref[0:bw, :])
           + _dot((yb * jax.nn.silu(proj(g0 + bw, bw))).astype(BF16), w_out_ref[bw:2 * bw, :])
           + _dot((yx * jax.nn.silu(proj(g0 + 2 * bw, XA_WIDTH))).astype(BF16), w_out_ref[2 * bw:, :]))
    o_ref[0] = x + _rms_norm(out, post_g_ref[...])


def _window_sums(ext_ref, s, tile, windows):
    sums = {}
    acc = ext_ref[s, pl.ds(C_HALO, tile), :]
    for j in range(1, max(windows)):
        acc = acc + ext_ref[s, pl.ds(C_HALO - j, tile), :]
        if j + 1 in windows:
            sums[j + 1] = acc
    return sums


_SLAB_WINDOWS = ((2,), (2, 4), (4,), (8,), (8, 16), (16,))


def _odd_kernel(x_ref, kt_ref, v_ref, inv_cnt_ref, never_ref, pre_g_ref, w_in_ref, wgrp_ref, c_scale_ref,
                dw_w_ref, dw_b_ref, ln_g_ref, ln_b_ref, pw_w_ref, pw_b_ref, w_out_ref, post_g_ref,
                o_ref, cext_ref, zext_ref):
    tile = x_ref.shape[1]
    bw = BRANCH_WIDTH

    @pl.when(pl.program_id(1) == 0)
    def _():
        _zero_halo(cext_ref, C_HALO)
        _zero_halo(zext_ref, D_HALO)

    x = x_ref[0]
    h = _rms_norm(x, pre_g_ref[...]).astype(BF16)
    g0 = 3 * bw + XA_WIDTH

    def proj(lo, width, lhs=None):
        if lhs is None:
            return _dot(h, w_in_ref[:, lo:lo + width])
        return _dot_pair(lhs, w_in_ref[:, lo:lo + width])

    z = proj(bw, bw) * jax.nn.sigmoid(proj(2 * bw, bw))
    for s in range(N_SLABS):
        zext_ref[s, D_HALO:D_HALO + tile, :] = _slab(z, s)
    later = [(0, bw), (3 * bw, XA_WIDTH), (g0, bw), (g0 + bw, bw), (g0 + 2 * bw, XA_WIDTH)]
    projected = {later[0]: proj(*later[0])}
    conv = []
    for s in range(N_SLABS):
        acc = None
        for k in range(CONF_CONV):
            term = zext_ref[s, pl.ds(D_HALO - (CONF_CONV - 1) + k, tile), :] * dw_w_ref[k:k + 1, s * LANES:(s + 1) * LANES]
            acc = term if acc is None else acc + term
        conv.append(acc)
        if s + 1 < len(later):
            projected[later[s + 1]] = proj(*later[s + 1], lhs=_after(h, never_ref, acc))
    _carry_halo(zext_ref, D_HALO, tile)

    zc = projected[(0, bw)]
    for s in range(N_SLABS):
        cext_ref[s, C_HALO:C_HALO + tile, :] = _slab(zc, s)
    low_half = lax.broadcasted_iota(jnp.int32, (tile, LANES), 1) < (LANES // 2)
    pooled = []
    for s in range(N_SLABS):
        sums = _window_sums(cext_ref, s, tile, _SLAB_WINDOWS[s])
        wins = _SLAB_WINDOWS[s]
        tot = sums[wins[0]] if len(wins) == 1 else jnp.where(low_half, sums[wins[0]], sums[wins[1]])
        inv = inv_cnt_ref[0, :, s * LANES:(s + 1) * LANES]
        mean = jnp.concatenate([tot[0:C_HALO] * inv, tot[C_HALO:] * inv[C_HALO - 1:C_HALO]], axis=0)
        pooled.append(mean - _slab(zc, s))
    _carry_halo(cext_ref, C_HALO, tile)
    pooled = jnp.concatenate(pooled, axis=-1).astype(BF16)
    half = bw // 2
    yc = jnp.concatenate([_dot(pooled[:, :half], wgrp_ref[0]), _dot(pooled[:, half:], wgrp_ref[1])],
                         axis=-1) * c_scale_ref[...]

    yx = _cross_attention(projected[(3 * bw, XA_WIDTH)], kt_ref, v_ref)

    zd = jnp.concatenate(conv, axis=-1) + dw_b_ref[...]
    zd = jax.nn.silu(_layer_norm(zd, ln_g_ref[...], ln_b_ref[...]))
    yd = _dot_pair(zd.astype(BF16), pw_w_ref[...]) + pw_b_ref[...]

    out = (_dot((yc * jax.nn.silu(projected[(g0, bw)])).astype(BF16), w_out_ref[0:bw, :])
           + _dot((yx * jax.nn.silu(projected[(g0 + 2 * bw, XA_WIDTH)])).astype(BF16), w_out_ref[2 * bw:, :])
           + _dot_pair((yd * jax.nn.silu(projected[(g0 + bw, bw)])).astype(BF16), w_out_ref[bw:2 * bw, :]))
    o_ref[0] = x + _rms_norm(out, post_g_ref[...])


def _kv_kernel(mem_ref, g_ref, wkt_ref, wv_ref, kt_ref, v_ref):
    mem_n = _rms_norm(mem_ref[0], g_ref[...]).astype(BF16)
    kt = lax.dot_general(wkt_ref[...], mem_n, (((1,), (1,)), ((), ())), preferred_element_type=F32)
    kt_ref[0] = kt.astype(BF16)
    v_ref[0] = _dot(mem_n, wv_ref[...]).astype(BF16)


def _memory_kv(mem, mem_g, w_kv):
    bsz = mem.shape[0]
    wkt = w_kv[:, :XA_WIDTH].T.astype(BF16)
    wv = w_kv[:, XA_WIDTH:].astype(BF16)
    const = lambda b: (0, 0)
    return pl.pallas_call(
        _kv_kernel,
        grid=(bsz,),
        in_specs=[pl.BlockSpec((1, N_MEM, D_MODEL), lambda b: (b, 0, 0)),
                  pl.BlockSpec((1, D_MODEL), const),
                  pl.BlockSpec((XA_WIDTH, D_MODEL), const),
                  pl.BlockSpec((D_MODEL, XA_WIDTH), const)],
        out_specs=[pl.BlockSpec((1, XA_WIDTH, N_MEM), lambda b: (b, 0, 0)),
                   pl.BlockSpec((1, N_MEM, XA_WIDTH), lambda b: (b, 0, 0))],
        out_shape=[jax.ShapeDtypeStruct((bsz, XA_WIDTH, N_MEM), BF16),
                   jax.ShapeDtypeStruct((bsz, N_MEM, XA_WIDTH), BF16)],
        compiler_params=pltpu.CompilerParams(dimension_semantics=("arbitrary",)),
        name="memory_kv",
    )(mem, mem_g.reshape(1, D_MODEL), wkt, wv)


def _resident(shape):
    zeros = (0,) * len(shape)
    return pl.BlockSpec(shape, lambda b, j: zeros, pipeline_mode=pl.Buffered(1))


def _row(p):
    return p.reshape(1, -1)


def _layer_call(body, name, tile, x, kt, v, extra_specs, extra_args, resident_args, scratch_shapes):
    bsz, seq, d = x.shape
    assert seq % tile == 0 and tile % CHUNK == 0
    tile_spec = pl.BlockSpec((1, tile, d), lambda b, j: (b, j, 0))
    in_specs = ([tile_spec,
                 pl.BlockSpec((1, XA_WIDTH, N_MEM), lambda b, j: (b, 0, 0)),
                 pl.BlockSpec((1, N_MEM, XA_WIDTH), lambda b, j: (b, 0, 0))]
                + extra_specs + [_resident(a.shape) for a in resident_args])
    return pl.pallas_call(
        body,
        grid=(bsz, seq // tile),
        in_specs=in_specs,
        out_specs=tile_spec,
        out_shape=jax.ShapeDtypeStruct(x.shape, x.dtype),
        scratch_shapes=scratch_shapes,
        compiler_params=pltpu.CompilerParams(dimension_semantics=("arbitrary", "arbitrary"),
                                             vmem_limit_bytes=VMEM_LIMIT_BYTES),
        name=name,
    )(x, kt, v, *extra_args, *resident_args)


def _even_layer(x, mem, pre_g, w_in, a_ln_g, a_ln_b, a_ws, a_bs, b_conv, mem_g, w_kv, w_out, post_g):
    kt, v = _memory_kv(mem, mem_g, w_kv)
    causal = jnp.tril(jnp.ones((CHUNK, CHUNK), dtype=bool))
    ws = jnp.where(causal[None], a_ws, 0.0).astype(BF16)
    bs = jnp.repeat(a_bs.T, A_HEAD_DIM, axis=1)
    resident = [_row(pre_g), w_in.astype(BF16), _row(a_ln_g), _row(a_ln_b), ws, bs, b_conv,
                w_out.astype(BF16), _row(post_g)]
    scratch = [pltpu.VMEM((N_SLABS, B_HALO + EVEN_TILE, LANES), F32)]
    return _layer_call(_even_kernel, "even_layer", EVEN_TILE, x, kt, v, [], [], resident, scratch)


def _odd_layer(x, mem, pre_g, w_in, c_wgrp, c_scale, d_dw_w, d_dw_b, d_ln_g, d_ln_b, d_pw_w, d_pw_b,
               mem_g, w_kv, w_out, post_g):
    kt, v = _memory_kv(mem, mem_g, w_kv)
    tile = ODD_TILE
    win = jnp.repeat(jnp.asarray(POOL_WINDOWS, F32), C_GROUP)
    pos = jnp.arange(1, C_HALO + 1, dtype=F32)[:, None]
    assert max(POOL_WINDOWS) <= C_HALO
    inv_cnt = jnp.stack([1.0 / jnp.minimum(pos, win[None, :]),
                         jnp.broadcast_to(1.0 / win[None, :], (C_HALO, BRANCH_WIDTH))])
    zero = jnp.zeros((C_GROUP, C_GROUP), c_wgrp.dtype)
    wgrp = jnp.stack([jnp.block([[c_wgrp[0], zero], [zero, c_wgrp[1]]]),
                      jnp.block([[c_wgrp[2], zero], [zero, c_wgrp[3]]])]).astype(BF16)
    extra_specs = [pl.BlockSpec((1, C_HALO, BRANCH_WIDTH), lambda b, j: (jnp.minimum(j, 1), 0, 0)),
                   pl.BlockSpec(memory_space=pltpu.SMEM)]
    never = jnp.zeros((1,), jnp.int32)
    resident = [_row(pre_g), w_in.astype(BF16), wgrp, _row(c_scale), d_dw_w, _row(d_dw_b),
                _row(d_ln_g), _row(d_ln_b), d_pw_w.astype(BF16), _row(d_pw_b), w_out.astype(BF16),
                _row(post_g)]
    scratch = [pltpu.VMEM((N_SLABS, C_HALO + tile, LANES), F32),
               pltpu.VMEM((N_SLABS, D_HALO + tile, LANES), F32)]
    return _layer_call(_odd_kernel, "odd_layer", tile, x, kt, v, extra_specs, [inv_cnt, never], resident, scratch)


def kernel(x, mem, even_pre_g, even_w_in, even_a_ln_g, even_a_ln_b, even_a_ws, even_a_bs, even_b_conv, even_mem_g, even_w_kv, even_w_out, even_post_g, odd_pre_g, odd_w_in, odd_c_wgrp, odd_c_scale, odd_d_dw_w, odd_d_dw_b, odd_d_ln_g, odd_d_ln_b, odd_d_pw_w, odd_d_pw_b, odd_mem_g, odd_w_kv, odd_w_out, odd_post_g):
    depth = even_pre_g.shape[0] + odd_pre_g.shape[0]
    for layer in range(depth):
        i = layer // 2
        if layer % 2 == 0:
            x = _even_layer(x, mem, even_pre_g[i], even_w_in[i], even_a_ln_g[i], even_a_ln_b[i],
                            even_a_ws[i], even_a_bs[i], even_b_conv[i], even_mem_g[i], even_w_kv[i],
                            even_w_out[i], even_post_g[i])
        else:
            x = _odd_layer(x, mem, odd_pre_g[i], odd_w_in[i], odd_c_wgrp[i], odd_c_scale[i],
                           odd_d_dw_w[i], odd_d_dw_b[i], odd_d_ln_g[i], odd_d_ln_b[i], odd_d_pw_w[i],
                           odd_d_pw_b[i], odd_mem_g[i], odd_w_kv[i], odd_w_out[i], odd_post_g[i])
    return x
```

```python
import functools

import jax
import jax.numpy as jnp
from jax import lax
from jax.experimental import pallas as pl
from jax.experimental.pallas import tpu as pltpu

D_MODEL = 1024
N_MEM = 256
MIX_WIDTH = 2 * D_MODEL
XA_HEADS = 4
XA_WIDTH = MIX_WIDTH // 4
XA_HEAD_DIM = XA_WIDTH // XA_HEADS
BRANCH_WIDTH = (MIX_WIDTH - XA_WIDTH) // 2
CHUNK = 128
A_HEADS = 4
A_HEAD_DIM = BRANCH_WIDTH // A_HEADS
SHORT_CONV = 3
POOL_WINDOWS = (2, 4, 8, 16)
C_GROUP = BRANCH_WIDTH // len(POOL_WINDOWS)
CONF_CONV = 31
EPS = 1e-6

LANES = 128
SUBLANES = 8
N_SLABS = BRANCH_WIDTH // LANES
EVEN_TILE = 1024
ODD_TILE = 512
FENCE_ROWS = 2 * SUBLANES
VMEM_LIMIT_BYTES = 56 * 1024 * 1024

B_HALO = SUBLANES
C_HALO = 2 * SUBLANES
D_HALO = 4 * SUBLANES

BF16 = jnp.bfloat16
F32 = jnp.float32


def _dot(a, b):
    return jnp.dot(a, b, preferred_element_type=F32)


def _dot_pair(a, b):
    half = a.shape[0] // 2
    return jnp.concatenate([_dot(a[:half], b), _dot(a[half:], b)], axis=0)


def _sigmoid(x):
    return 0.5 * jnp.tanh(0.5 * x) + 0.5


def _silu(x):
    return x * _sigmoid(x)


def _rms_norm(x, g):
    y = x * lax.rsqrt(jnp.mean(x * x, axis=-1, keepdims=True) + EPS)
    return y * g


def _layer_norm(x, g, b):
    mu = jnp.mean(x, axis=-1, keepdims=True)
    xc = x - mu
    var = jnp.mean(xc * xc, axis=-1, keepdims=True)
    return xc * lax.rsqrt(var + EPS) * g + b


def _slab(v, s):
    return v[:, s * LANES:(s + 1) * LANES]


def _cross_attention(q, kt_ref, v_ref):
    qb = q.astype(BF16)
    outs = []
    for hd in range(XA_HEADS):
        lo, hi = hd * XA_HEAD_DIM, (hd + 1) * XA_HEAD_DIM
        s = _dot(qb[:, lo:hi], kt_ref[0, lo:hi, :]) * (XA_HEAD_DIM ** -0.5)
        m = jnp.max(s, axis=-1, keepdims=True)
        e = jnp.exp(s - m)
        p = e / jnp.sum(e, axis=-1, keepdims=True)
        outs.append(_dot(p.astype(BF16), v_ref[0, :, lo:hi]))
    return jnp.concatenate(outs, axis=-1)


def _after(h, never_ref, value):
    rows, cols = value.shape
    folded = value.reshape(rows // FENCE_ROWS, FENCE_ROWS, cols).sum(axis=0)
    folded = sum(folded[:, c:c + LANES] for c in range(0, cols, LANES))
    first = jnp.where(never_ref[0] != 0, folded.astype(BF16), h[0:FENCE_ROWS, 0:LANES])
    top = jnp.concatenate([first, h[0:FENCE_ROWS, LANES:]], axis=1)
    return jnp.concatenate([top, h[FENCE_ROWS:]], axis=0)


def _carry_halo(ext_ref, halo, tile):
    for s in range(N_SLABS):
        ext_ref[s, 0:halo, :] = ext_ref[s, tile:tile + halo, :]


def _zero_halo(ext_ref, halo):
    ext_ref[:, 0:halo, :] = jnp.zeros((N_SLABS, halo, LANES), F32)


def _even_kernel(x_ref, kt_ref, v_ref, pre_g_ref, w_in_ref, ln_g_ref, ln_b_ref, ws_ref, bs_ref,
                 bconv_ref, w_out_ref, post_g_ref, o_ref, bext_ref):
    tile = x_ref.shape[1]
    bw = BRANCH_WIDTH

    @pl.when(pl.program_id(1) == 0)
    def _():
        _zero_halo(bext_ref, B_HALO)

    x = x_ref[0]
    h = _rms_norm(x, pre_g_ref[...]).astype(BF16)

    def proj(lo, width):
        return _dot(h, w_in_ref[:, lo:lo + width])

    u = proj(0, bw)
    v = proj(bw, bw)
    vn = _layer_norm(v, ln_g_ref[...], ln_b_ref[...]).astype(BF16)
    low_half = lax.broadcasted_iota(jnp.int32, (CHUNK, LANES), 1) < (LANES // 2)
    sg_chunks = []
    for c in range(tile // CHUNK):
        rows = vn[c * CHUNK:(c + 1) * CHUNK, :]
        r0 = _dot(ws_ref[0], rows[:, 0:256])
        r1 = _dot(ws_ref[1], rows[:, 128:384])
        r2 = _dot(ws_ref[2], rows[:, 384:640])
        r3 = _dot(ws_ref[3], rows[:, 512:768])
        sg_chunks.append(jnp.concatenate([
            r0[:, 0:128], jnp.where(low_half, r0[:, 128:256], r1[:, 0:128]), r1[:, 128:256],
            r2[:, 0:128], jnp.where(low_half, r2[:, 128:256], r3[:, 0:128]), r3[:, 128:256]],
            axis=-1) + bs_ref[...])
    ya = u * jnp.concatenate(sg_chunks, axis=0)

    bg = proj(2 * bw, bw)
    cx = proj(3 * bw, bw) * proj(4 * bw, bw)
    for s in range(N_SLABS):
        bext_ref[s, B_HALO:B_HALO + tile, :] = _slab(cx, s)
    conv = []
    for s in range(N_SLABS):
        acc = None
        for k in range(SHORT_CONV):
            term = bext_ref[s, pl.ds(B_HALO - (SHORT_CONV - 1) + k, tile), :] * bconv_ref[k:k + 1, s * LANES:(s + 1) * LANES]
            acc = term if acc is None else acc + term
        conv.append(acc)
    yb = bg * jnp.concatenate(conv, axis=-1)
    _carry_halo(bext_ref, B_HALO, tile)

    yx = _cross_attention(proj(5 * bw, XA_WIDTH), kt_ref, v_ref)

    g0 = 5 * bw + XA_WIDTH
    out = (_dot((ya * _silu(proj(g0, bw))).astype(BF16), w_out_ref[0:bw, :])
           + _dot((yb * _silu(proj(g0 + bw, bw))).astype(BF16), w_out_ref[bw:2 * bw, :])
           + _dot((yx * _silu(proj(g0 + 2 * bw, XA_WIDTH))).astype(BF16), w_out_ref[2 * bw:, :]))
    o_ref[0] = x + _rms_norm(out, post_g_ref[...])


def _window_sums(ext_ref, s, tile, windows):
    sums = {}
    acc = ext_ref[s, pl.ds(C_HALO, tile), :]
    for j in range(1, max(windows)):
        acc = acc + ext_ref[s, pl.ds(C_HALO - j, tile), :]
        if j + 1 in windows:
            sums[j + 1] = acc
    return sums


_SLAB_WINDOWS = ((2,), (2, 4), (4,), (8,), (8, 16), (16,))


def _odd_kernel(x_ref, kt_ref, v_ref, inv_cnt_ref, never_ref, pre_g_ref, w_in_ref, wgrp_ref, c_scale_ref,
                dw_w_ref, dw_b_ref, ln_g_ref, ln_b_ref, pw_w_ref, pw_b_ref, w_out_ref, post_g_ref,
                o_ref, cext_ref, zext_ref):
    tile = x_ref.shape[1]
    bw = BRANCH_WIDTH

    @pl.when(pl.program_id(1) == 0)
    def _():
        _zero_halo(cext_ref, C_HALO)
        _zero_halo(zext_ref, D_HALO)

    x = x_ref[0]
    h = _rms_norm(x, pre_g_ref[...]).astype(BF16)
    g0 = 3 * bw + XA_WIDTH

    def proj(lo, width, lhs=None):
        return _dot_pair(h if lhs is None else lhs, w_in_ref[:, lo:lo + width])

    z = proj(bw, bw) * _sigmoid(proj(2 * bw, bw))
    for s in range(N_SLABS):
        zext_ref[s, D_HALO:D_HALO + tile, :] = _slab(z, s)
    def conv_slab(s):
        acc = None
        for k in range(CONF_CONV):
            term = zext_ref[s, pl.ds(D_HALO - (CONF_CONV - 1) + k, tile), :] * dw_w_ref[k:k + 1, s * LANES:(s + 1) * LANES]
            acc = term if acc is None else acc + term
        return acc

    zc = proj(0, bw)
    q = proj(3 * bw, XA_WIDTH)
    conv = [conv_slab(0)]
    gate_c = proj(g0, bw, lhs=_after(h, never_ref, conv[0]))

    for s in range(N_SLABS):
        cext_ref[s, C_HALO:C_HALO + tile, :] = _slab(zc, s)
    low_half = lax.broadcasted_iota(jnp.int32, (tile, LANES), 1) < (LANES // 2)
    pooled = []
    for s in range(N_SLABS):
        sums = _window_sums(cext_ref, s, tile, _SLAB_WINDOWS[s])
        wins = _SLAB_WINDOWS[s]
        tot = sums[wins[0]] if len(wins) == 1 else jnp.where(low_half, sums[wins[0]], sums[wins[1]])
        inv = inv_cnt_ref[0, :, s * LANES:(s + 1) * LANES]
        mean = jnp.concatenate([tot[0:C_HALO] * inv, tot[C_HALO:] * inv[C_HALO - 1:C_HALO]], axis=0)
        pooled.append(mean - _slab(zc, s))
    _carry_halo(cext_ref, C_HALO, tile)
    pooled = jnp.concatenate(pooled, axis=-1).astype(BF16)
    half = bw // 2
    yc = jnp.concatenate([_dot(pooled[:, :half], wgrp_ref[0]), _dot(pooled[:, half:], wgrp_ref[1])],
                         axis=-1) * c_scale_ref[...]

    conv.append(conv_slab(1))
    gate_d = proj(g0 + bw, bw, lhs=_after(h, never_ref, conv[1]))

    yx = _cross_attention(q, kt_ref, v_ref)

    conv.append(conv_slab(2))
    gate_x = proj(g0 + 2 * bw, XA_WIDTH, lhs=_after(h, never_ref, conv[2]))

    conv.append(conv_slab(3))
    yc_g = (yc * _silu(gate_c)).astype(BF16)
    out = _dot_pair(_after(yc_g, never_ref, conv[3]), w_out_ref[0:bw, :])
    conv.append(conv_slab(4))
    yx_g = (yx * _silu(gate_x)).astype(BF16)
    out = out + _dot_pair(_after(yx_g, never_ref, conv[4]), w_out_ref[2 * bw:, :])
    conv.append(conv_slab(5))
    _carry_halo(zext_ref, D_HALO, tile)

    zd = jnp.concatenate(conv, axis=-1) + dw_b_ref[...]
    zd = _silu(_layer_norm(zd, ln_g_ref[...], ln_b_ref[...]))
    yd = _dot_pair(zd.astype(BF16), pw_w_ref[...]) + pw_b_ref[...]
    out = out + _dot_pair((yd * _silu(gate_d)).astype(BF16), w_out_ref[bw:2 * bw, :])
    o_ref[0] = x + _rms_norm(out, post_g_ref[...])


def _kv_kernel(mem_ref, g_ref, wkt_ref, wv_ref, kt_ref, v_ref):
    mem_n = _rms_norm(mem_ref[0], g_ref[...]).astype(BF16)
    kt = lax.dot_general(wkt_ref[...], mem_n, (((1,), (1,)), ((), ())), preferred_element_type=F32)
    kt_ref[0] = kt.astype(BF16)
    v_ref[0] = _dot(mem_n, wv_ref[...]).astype(BF16)


def _memory_kv(mem, mem_g, w_kv):
    bsz = mem.shape[0]
    wkt = w_kv[:, :XA_WIDTH].T.astype(BF16)
    wv = w_kv[:, XA_WIDTH:].astype(BF16)
    const = lambda b: (0, 0)
    return pl.pallas_call(
        _kv_kernel,
        grid=(bsz,),
        in_specs=[pl.BlockSpec((1, N_MEM, D_MODEL), lambda b: (b, 0, 0)),
                  pl.BlockSpec((1, D_MODEL), const),
                  pl.BlockSpec((XA_WIDTH, D_MODEL), const),
                  pl.BlockSpec((D_MODEL, XA_WIDTH), const)],
        out_specs=[pl.BlockSpec((1, XA_WIDTH, N_MEM), lambda b: (b, 0, 0)),
                   pl.BlockSpec((1, N_MEM, XA_WIDTH), lambda b: (b, 0, 0))],
        out_shape=[jax.ShapeDtypeStruct((bsz, XA_WIDTH, N_MEM), BF16),
                   jax.ShapeDtypeStruct((bsz, N_MEM, XA_WIDTH), BF16)],
        compiler_params=pltpu.CompilerParams(dimension_semantics=("arbitrary",)),
        name="memory_kv",
    )(mem, mem_g.reshape(1, D_MODEL), wkt, wv)


def _resident(shape):
    zeros = (0,) * len(shape)
    return pl.BlockSpec(shape, lambda b, j: zeros, pipeline_mode=pl.Buffered(1))


def _row(p):
    return p.reshape(1, -1)


def _layer_call(body, name, tile, x, kt, v, extra_specs, extra_args, resident_args, scratch_shapes):
    bsz, seq, d = x.shape
    assert seq % tile == 0 and tile % CHUNK == 0
    tile_spec = pl.BlockSpec((1, tile, d), lambda b, j: (b, j, 0))
    in_specs = ([tile_spec,
                 pl.BlockSpec((1, XA_WIDTH, N_MEM), lambda b, j: (b, 0, 0)),
                 pl.BlockSpec((1, N_MEM, XA_WIDTH), lambda b, j: (b, 0, 0))]
                + extra_specs + [_resident(a.shape) for a in resident_args])
    return pl.pallas_call(
        body,
        grid=(bsz, seq // tile),
        in_specs=in_specs,
        out_specs=tile_spec,
        out_shape=jax.ShapeDtypeStruct(x.shape, x.dtype),
        scratch_shapes=scratch_shapes,
        compiler_params=pltpu.CompilerParams(dimension_semantics=("arbitrary", "arbitrary"),
                                             vmem_limit_bytes=VMEM_LIMIT_BYTES),
        name=name,
    )(x, kt, v, *extra_args, *resident_args)


def _even_layer(x, mem, pre_g, w_in, a_ln_g, a_ln_b, a_ws, a_bs, b_conv, mem_g, w_kv, w_out, post_g):
    kt, v = _memory_kv(mem, mem_g, w_kv)
    causal = jnp.tril(jnp.ones((CHUNK, CHUNK), dtype=bool))
    ws = jnp.where(causal[None], a_ws, 0.0).astype(BF16)
    bs = jnp.repeat(a_bs.T, A_HEAD_DIM, axis=1)
    resident = [_row(pre_g), w_in.astype(BF16), _row(a_ln_g), _row(a_ln_b), ws, bs, b_conv,
                w_out.astype(BF16), _row(post_g)]
    scratch = [pltpu.VMEM((N_SLABS, B_HALO + EVEN_TILE, LANES), F32)]
    return _layer_call(_even_kernel, "even_layer", EVEN_TILE, x, kt, v, [], [], resident, scratch)


def _odd_layer(x, mem, pre_g, w_in, c_wgrp, c_scale, d_dw_w, d_dw_b, d_ln_g, d_ln_b, d_pw_w, d_pw_b,
               mem_g, w_kv, w_out, post_g):
    kt, v = _memory_kv(mem, mem_g, w_kv)
    tile = ODD_TILE
    win = jnp.repeat(jnp.asarray(POOL_WINDOWS, F32), C_GROUP)
    pos = jnp.arange(1, C_HALO + 1, dtype=F32)[:, None]
    assert max(POOL_WINDOWS) <= C_HALO
    inv_cnt = jnp.stack([1.0 / jnp.minimum(pos, win[None, :]),
                         jnp.broadcast_to(1.0 / win[None, :], (C_HALO, BRANCH_WIDTH))])
    zero = jnp.zeros((C_GROUP, C_GROUP), c_wgrp.dtype)
    wgrp = jnp.stack([jnp.block([[c_wgrp[0], zero], [zero, c_wgrp[1]]]),
                      jnp.block([[c_wgrp[2], zero], [zero, c_wgrp[3]]])]).astype(BF16)
    extra_specs = [pl.BlockSpec((1, C_HALO, BRANCH_WIDTH), lambda b, j: (jnp.minimum(j, 1), 0, 0)),
                   pl.BlockSpec(memory_space=pltpu.SMEM)]
    never = jnp.zeros((1,), jnp.int32)
    resident = [_row(pre_g), w_in.astype(BF16), wgrp, _row(c_scale), d_dw_w, _row(d_dw_b),
                _row(d_ln_g), _row(d_ln_b), d_pw_w.astype(BF16), _row(d_pw_b), w_out.astype(BF16),
                _row(post_g)]
    scratch = [pltpu.VMEM((N_SLABS, C_HALO + tile, LANES), F32),
               pltpu.VMEM((N_SLABS, D_HALO + tile, LANES), F32)]
    return _layer_call(_odd_kernel, "odd_layer", tile, x, kt, v, extra_specs, [inv_cnt, never], resident, scratch)


def kernel(x, mem, even_pre_g, even_w_in, even_a_ln_g, even_a_ln_b, even_a_ws, even_a_bs, even_b_conv, even_mem_g, even_w_kv, even_w_out, even_post_g, odd_pre_g, odd_w_in, odd_c_wgrp, odd_c_scale, odd_d_dw_w, odd_d_dw_b, odd_d_ln_g, odd_d_ln_b, odd_d_pw_w, odd_d_pw_b, odd_mem_g, odd_w_kv, odd_w_out, odd_post_g):
    depth = even_pre_g.shape[0] + odd_pre_g.shape[0]
    for layer in range(depth):
        i = layer // 2
        if layer % 2 == 0:
            x = _even_layer(x, mem, even_pre_g[i], even_w_in[i], even_a_ln_g[i], even_a_ln_b[i],
                            even_a_ws[i], even_a_bs[i], even_b_conv[i], even_mem_g[i], even_w_kv[i],
                            even_w_out[i], even_post_g[i])
        else:
            x = _odd_layer(x, mem, odd_pre_g[i], odd_w_in[i], odd_c_wgrp[i], odd_c_scale[i],
                           odd_d_dw_w[i], odd_d_dw_b[i], odd_d_ln_g[i], odd_d_ln_b[i], odd_d_pw_w[i],
                           odd_d_pw_b[i], odd_mem_g[i], odd_w_kv[i], odd_w_out[i], odd_post_g[i])
    return x
```

```python
import functools

import jax
import jax.numpy as jnp
from jax import lax
from jax.experimental import pallas as pl
from jax.experimental.pallas import tpu as pltpu

D_MODEL = 1024
N_MEM = 256
MIX_WIDTH = 2 * D_MODEL
XA_HEADS = 4
XA_WIDTH = MIX_WIDTH // 4
XA_HEAD_DIM = XA_WIDTH // XA_HEADS
BRANCH_WIDTH = (MIX_WIDTH - XA_WIDTH) // 2
CHUNK = 128
A_HEADS = 4
A_HEAD_DIM = BRANCH_WIDTH // A_HEADS
SHORT_CONV = 3
POOL_WINDOWS = (2, 4, 8, 16)
C_GROUP = BRANCH_WIDTH // len(POOL_WINDOWS)
CONF_CONV = 31
EPS = 1e-6

LANES = 128
SUBLANES = 8
N_SLABS = BRANCH_WIDTH // LANES
EVEN_TILE = 1024
ODD_TILE = 512
FENCE_ROWS = 2 * SUBLANES
VMEM_LIMIT_BYTES = 56 * 1024 * 1024

B_HALO = SUBLANES
C_HALO = 2 * SUBLANES
D_HALO = 4 * SUBLANES

BF16 = jnp.bfloat16
F32 = jnp.float32


def _dot(a, b):
    return jnp.dot(a, b, preferred_element_type=F32)


def _dot_pair(a, b):
    half = a.shape[0] // 2
    return jnp.concatenate([_dot(a[:half], b), _dot(a[half:], b)], axis=0)


def _sigmoid(x):
    return 0.5 * jnp.tanh(0.5 * x) + 0.5


def _silu(x):
    return x * _sigmoid(x)


def _rms_norm(x, g):
    y = x * lax.rsqrt(jnp.mean(x * x, axis=-1, keepdims=True) + EPS)
    return y * g


def _layer_norm(x, g, b):
    mu = jnp.mean(x, axis=-1, keepdims=True)
    xc = x - mu
    var = jnp.mean(xc * xc, axis=-1, keepdims=True)
    return xc * lax.rsqrt(var + EPS) * g + b


def _slab(v, s):
    return v[:, s * LANES:(s + 1) * LANES]


def _cross_attention(q, kt_ref, v_ref, dot):
    qb = q.astype(BF16)
    outs = []
    for hd in range(XA_HEADS):
        lo, hi = hd * XA_HEAD_DIM, (hd + 1) * XA_HEAD_DIM
        s = dot(qb[:, lo:hi], kt_ref[0, lo:hi, :]) * (XA_HEAD_DIM ** -0.5)
        m = jnp.max(s, axis=-1, keepdims=True)
        e = jnp.exp(s - m)
        p = e / jnp.sum(e, axis=-1, keepdims=True)
        outs.append(dot(p.astype(BF16), v_ref[0, :, lo:hi]))
    return jnp.concatenate(outs, axis=-1)


def _after(h, never_ref, value):
    rows, cols = value.shape
    folded = value.reshape(rows // FENCE_ROWS, FENCE_ROWS, cols).sum(axis=0)
    folded = sum(folded[:, c:c + LANES] for c in range(0, cols, LANES))
    first = jnp.where(never_ref[0] != 0, folded.astype(BF16), h[0:FENCE_ROWS, 0:LANES])
    top = jnp.concatenate([first, h[0:FENCE_ROWS, LANES:]], axis=1)
    return jnp.concatenate([top, h[FENCE_ROWS:]], axis=0)


def _carry_halo(ext_ref, halo, tile):
    for s in range(N_SLABS):
        ext_ref[s, 0:halo, :] = ext_ref[s, tile:tile + halo, :]


def _zero_halo(ext_ref, halo):
    ext_ref[:, 0:halo, :] = jnp.zeros((N_SLABS, halo, LANES), F32)


def _even_kernel(x_ref, kt_ref, v_ref, pre_g_ref, w_in_ref, ln_g_ref, ln_b_ref, ws_ref, bs_ref,
                 bconv_ref, w_out_ref, post_g_ref, o_ref, bext_ref):
    tile = x_ref.shape[1]
    bw = BRANCH_WIDTH

    @pl.when(pl.program_id(1) == 0)
    def _():
        _zero_halo(bext_ref, B_HALO)

    x = x_ref[0]
    h = _rms_norm(x, pre_g_ref[...]).astype(BF16)

    def proj(lo, width):
        return _dot(h, w_in_ref[:, lo:lo + width])

    u = proj(0, bw)
    v = proj(bw, bw)
    vn = _layer_norm(v, ln_g_ref[...], ln_b_ref[...]).astype(BF16)
    low_half = lax.broadcasted_iota(jnp.int32, (CHUNK, LANES), 1) < (LANES // 2)
    sg_chunks = []
    for c in range(tile // CHUNK):
        rows = vn[c * CHUNK:(c + 1) * CHUNK, :]
        r0 = _dot(ws_ref[0], rows[:, 0:256])
        r1 = _dot(ws_ref[1], rows[:, 128:384])
        r2 = _dot(ws_ref[2], rows[:, 384:640])
        r3 = _dot(ws_ref[3], rows[:, 512:768])
        sg_chunks.append(jnp.concatenate([
            r0[:, 0:128], jnp.where(low_half, r0[:, 128:256], r1[:, 0:128]), r1[:, 128:256],
            r2[:, 0:128], jnp.where(low_half, r2[:, 128:256], r3[:, 0:128]), r3[:, 128:256]],
            axis=-1) + bs_ref[...])
    ya = u * jnp.concatenate(sg_chunks, axis=0)

    bg = proj(2 * bw, bw)
    cx = proj(3 * bw, bw) * proj(4 * bw, bw)
    for s in range(N_SLABS):
        bext_ref[s, B_HALO:B_HALO + tile, :] = _slab(cx, s)
    conv = []
    for s in range(N_SLABS):
        acc = None
        for k in range(SHORT_CONV):
            term = bext_ref[s, pl.ds(B_HALO - (SHORT_CONV - 1) + k, tile), :] * bconv_ref[k:k + 1, s * LANES:(s + 1) * LANES]
            acc = term if acc is None else acc + term
        conv.append(acc)
    yb = bg * jnp.concatenate(conv, axis=-1)
    _carry_halo(bext_ref, B_HALO, tile)

    yx = _cross_attention(proj(5 * bw, XA_WIDTH), kt_ref, v_ref, _dot)

    g0 = 5 * bw + XA_WIDTH
    out = (_dot((ya * _silu(proj(g0, bw))).astype(BF16), w_out_ref[0:bw, :])
           + _dot((yb * _silu(proj(g0 + bw, bw))).astype(BF16), w_out_ref[bw:2 * bw, :])
           + _dot((yx * _silu(proj(g0 + 2 * bw, XA_WIDTH))).astype(BF16), w_out_ref[2 * bw:, :]))
    o_ref[0] = x + _rms_norm(out, post_g_ref[...])


def _window_sums(ext_ref, s, tile, windows):
    sums = {}
    acc = ext_ref[s, pl.ds(C_HALO, tile), :]
    for j in range(1, max(windows)):
        acc = acc + ext_ref[s, pl.ds(C_HALO - j, tile), :]
        if j + 1 in windows:
            sums[j + 1] = acc
    return sums


_SLAB_WINDOWS = ((2,), (2, 4), (4,), (8,), (8, 16), (16,))


def _odd_kernel(x_ref, kt_ref, v_ref, inv_cnt_ref, never_ref, pre_g_ref, w_in_ref, wgrp_ref, c_scale_ref,
                dw_w_ref, dw_b_ref, ln_g_ref, ln_b_ref, pw_w_ref, pw_b_ref, w_out_ref, post_g_ref,
                o_ref, cext_ref, zext_ref):
    tile = x_ref.shape[1]
    bw = BRANCH_WIDTH

    @pl.when(pl.program_id(1) == 0)
    def _():
        _zero_halo(cext_ref, C_HALO)
        _zero_halo(zext_ref, D_HALO)

    x = x_ref[0]
    h = _rms_norm(x, pre_g_ref[...]).astype(BF16)
    g0 = 3 * bw + XA_WIDTH

    def proj(lo, width, lhs=None):
        return _dot_pair(h if lhs is None else lhs, w_in_ref[:, lo:lo + width])

    z = proj(bw, bw) * _sigmoid(proj(2 * bw, bw))
    for s in range(N_SLABS):
        zext_ref[s, D_HALO:D_HALO + tile, :] = _slab(z, s)
    def conv_slab(s):
        acc = None
        for k in range(CONF_CONV):
            term = zext_ref[s, pl.ds(D_HALO - (CONF_CONV - 1) + k, tile), :] * dw_w_ref[k:k + 1, s * LANES:(s + 1) * LANES]
            acc = term if acc is None else acc + term
        return acc

    zc = proj(0, bw)
    q = proj(3 * bw, XA_WIDTH)
    conv = [conv_slab(0)]
    gate_c = proj(g0, bw, lhs=_after(h, never_ref, conv[0]))

    for s in range(N_SLABS):
        cext_ref[s, C_HALO:C_HALO + tile, :] = _slab(zc, s)
    low_half = lax.broadcasted_iota(jnp.int32, (tile, LANES), 1) < (LANES // 2)
    pooled = []
    for s in range(N_SLABS):
        sums = _window_sums(cext_ref, s, tile, _SLAB_WINDOWS[s])
        wins = _SLAB_WINDOWS[s]
        tot = sums[wins[0]] if len(wins) == 1 else jnp.where(low_half, sums[wins[0]], sums[wins[1]])
        inv = inv_cnt_ref[0, :, s * LANES:(s + 1) * LANES]
        mean = jnp.concatenate([tot[0:C_HALO] * inv, tot[C_HALO:] * inv[C_HALO - 1:C_HALO]], axis=0)
        pooled.append(mean - _slab(zc, s))
    _carry_halo(cext_ref, C_HALO, tile)
    pooled = jnp.concatenate(pooled, axis=-1).astype(BF16)
    half = bw // 2
    yc = jnp.concatenate([_dot(pooled[:, :half], wgrp_ref[0]), _dot(pooled[:, half:], wgrp_ref[1])],
                         axis=-1) * c_scale_ref[...]

    conv.append(conv_slab(1))
    gate_d = proj(g0 + bw, bw, lhs=_after(h, never_ref, conv[1]))

    yx = _cross_attention(q, kt_ref, v_ref, _dot_pair)

    conv.append(conv_slab(2))
    gate_x = proj(g0 + 2 * bw, XA_WIDTH, lhs=_after(h, never_ref, conv[2]))

    conv.append(conv_slab(3))
    yc_g = (yc * _silu(gate_c)).astype(BF16)
    out = _dot_pair(_after(yc_g, never_ref, conv[3]), w_out_ref[0:bw, :])
    conv.append(conv_slab(4))
    yx_g = (yx * _silu(gate_x)).astype(BF16)
    out = out + _dot_pair(_after(yx_g, never_ref, conv[4]), w_out_ref[2 * bw:, :])
    conv.append(conv_slab(5))
    _carry_halo(zext_ref, D_HALO, tile)

    zd = jnp.concatenate(conv, axis=-1) + dw_b_ref[...]
    zd = _silu(_layer_norm(zd, ln_g_ref[...], ln_b_ref[...]))
    yd = _dot_pair(zd.astype(BF16), pw_w_ref[...]) + pw_b_ref[...]
    out = out + _dot_pair((yd * _silu(gate_d)).astype(BF16), w_out_ref[bw:2 * bw, :])
    o_ref[0] = x + _rms_norm(out, post_g_ref[...])


def _kv_kernel(mem_ref, g_ref, wkt_ref, wv_ref, kt_ref, v_ref):
    mem_n = _rms_norm(mem_ref[0], g_ref[...]).astype(BF16)
    kt = lax.dot_general(wkt_ref[...], mem_n, (((1,), (1,)), ((), ())), preferred_element_type=F32)
    kt_ref[0] = kt.astype(BF16)
    v_ref[0] = _dot(mem_n, wv_ref[...]).astype(BF16)


def _memory_kv(mem, mem_g, w_kv):
    bsz = mem.shape[0]
    wkt = w_kv[:, :XA_WIDTH].T.astype(BF16)
    wv = w_kv[:, XA_WIDTH:].astype(BF16)
    const = lambda b: (0, 0)
    return pl.pallas_call(
        _kv_kernel,
        grid=(bsz,),
        in_specs=[pl.BlockSpec((1, N_MEM, D_MODEL), lambda b: (b, 0, 0)),
                  pl.BlockSpec((1, D_MODEL), const),
                  pl.BlockSpec((XA_WIDTH, D_MODEL), const),
                  pl.BlockSpec((D_MODEL, XA_WIDTH), const)],
        out_specs=[pl.BlockSpec((1, XA_WIDTH, N_MEM), lambda b: (b, 0, 0)),
                   pl.BlockSpec((1, N_MEM, XA_WIDTH), lambda b: (b, 0, 0))],
        out_shape=[jax.ShapeDtypeStruct((bsz, XA_WIDTH, N_MEM), BF16),
                   jax.ShapeDtypeStruct((bsz, N_MEM, XA_WIDTH), BF16)],
        compiler_params=pltpu.CompilerParams(dimension_semantics=("arbitrary",)),
        name="memory_kv",
    )(mem, mem_g.reshape(1, D_MODEL), wkt, wv)


def _resident(shape):
    zeros = (0,) * len(shape)
    return pl.BlockSpec(shape, lambda b, j: zeros, pipeline_mode=pl.Buffered(1))


def _row(p):
    return p.reshape(1, -1)


def _layer_call(body, name, tile, x, kt, v, extra_specs, extra_args, resident_args, scratch_shapes):
    bsz, seq, d = x.shape
    assert seq % tile == 0 and tile % CHUNK == 0
    tile_spec = pl.BlockSpec((1, tile, d), lambda b, j: (b, j, 0))
    in_specs = ([tile_spec,
                 pl.BlockSpec((1, XA_WIDTH, N_MEM), lambda b, j: (b, 0, 0)),
                 pl.BlockSpec((1, N_MEM, XA_WIDTH), lambda b, j: (b, 0, 0))]
                + extra_specs + [_resident(a.shape) for a in resident_args])
    return pl.pallas_call(
        body,
        grid=(bsz, seq // tile),
        in_specs=in_specs,
        out_specs=tile_spec,
        out_shape=jax.ShapeDtypeStruct(x.shape, x.dtype),
        scratch_shapes=scratch_shapes,
        compiler_params=pltpu.CompilerParams(dimension_semantics=("arbitrary", "arbitrary"),
                                             vmem_limit_bytes=VMEM_LIMIT_BYTES),
        name=name,
    )(x, kt, v, *extra_args, *resident_args)


def _even_layer(x, mem, pre_g, w_in, a_ln_g, a_ln_b, a_ws, a_bs, b_conv, mem_g, w_kv, w_out, post_g):
    kt, v = _memory_kv(mem, mem_g, w_kv)
    causal = jnp.tril(jnp.ones((CHUNK, CHUNK), dtype=bool))
    ws = jnp.where(causal[None], a_ws, 0.0).astype(BF16)
    bs = jnp.repeat(a_bs.T, A_HEAD_DIM, axis=1)
    resident = [_row(pre_g), w_in.astype(BF16), _row(a_ln_g), _row(a_ln_b), ws, bs, b_conv,
                w_out.astype(BF16), _row(post_g)]
    scratch = [pltpu.VMEM((N_SLABS, B_HALO + EVEN_TILE, LANES), F32)]
    return _layer_call(_even_kernel, "even_layer", EVEN_TILE, x, kt, v, [], [], resident, scratch)


def _odd_layer(x, mem, pre_g, w_in, c_wgrp, c_scale, d_dw_w, d_dw_b, d_ln_g, d_ln_b, d_pw_w, d_pw_b,
               mem_g, w_kv, w_out, post_g):
    kt, v = _memory_kv(mem, mem_g, w_kv)
    tile = ODD_TILE
    win = jnp.repeat(jnp.asarray(POOL_WINDOWS, F32), C_GROUP)
    pos = jnp.arange(1, C_HALO + 1, dtype=F32)[:, None]
    assert max(POOL_WINDOWS) <= C_HALO
    inv_cnt = jnp.stack([1.0 / jnp.minimum(pos, win[None, :]),
                         jnp.broadcast_to(1.0 / win[None, :], (C_HALO, BRANCH_WIDTH))])
    zero = jnp.zeros((C_GROUP, C_GROUP), c_wgrp.dtype)
    wgrp = jnp.stack([jnp.block([[c_wgrp[0], zero], [zero, c_wgrp[1]]]),
                      jnp.block([[c_wgrp[2], zero], [zero, c_wgrp[3]]])]).astype(BF16)
    extra_specs = [pl.BlockSpec((1, C_HALO, BRANCH_WIDTH), lambda b, j: (jnp.minimum(j, 1), 0, 0)),
                   pl.BlockSpec(memory_space=pltpu.SMEM)]
    never = jnp.zeros((1,), jnp.int32)
    resident = [_row(pre_g), w_in.astype(BF16), wgrp, _row(c_scale), d_dw_w, _row(d_dw_b),
                _row(d_ln_g), _row(d_ln_b), d_pw_w.astype(BF16), _row(d_pw_b), w_out.astype(BF16),
                _row(post_g)]
    scratch = [pltpu.VMEM((N_SLABS, C_HALO + tile, LANES), F32),
               pltpu.VMEM((N_SLABS, D_HALO + tile, LANES), F32)]
    return _layer_call(_odd_kernel, "odd_layer", tile, x, kt, v, extra_specs, [inv_cnt, never], resident, scratch)


def kernel(x, mem, even_pre_g, even_w_in, even_a_ln_g, even_a_ln_b, even_a_ws, even_a_bs, even_b_conv, even_mem_g, even_w_kv, even_w_out, even_post_g, odd_pre_g, odd_w_in, odd_c_wgrp, odd_c_scale, odd_d_dw_w, odd_d_dw_b, odd_d_ln_g, odd_d_ln_b, odd_d_pw_w, odd_d_pw_b, odd_mem_g, odd_w_kv, odd_w_out, odd_post_g):
    depth = even_pre_g.shape[0] + odd_pre_g.shape[0]
    for layer in range(depth):
        i = layer // 2
        if layer % 2 == 0:
            x = _even_layer(x, mem, even_pre_g[i], even_w_in[i], even_a_ln_g[i], even_a_ln_b[i],
                            even_a_ws[i], even_a_bs[i], even_b_conv[i], even_mem_g[i], even_w_kv[i],
                            even_w_out[i], even_post_g[i])
        else:
            x = _odd_layer(x, mem, odd_pre_g[i], odd_w_in[i], odd_c_wgrp[i], odd_c_scale[i],
                           odd_d_dw_w[i], odd_d_dw_b[i], odd_d_ln_g[i], odd_d_ln_b[i], odd_d_pw_w[i],
                           odd_d_pw_b[i], odd_mem_g[i], odd_w_kv[i], odd_w_out[i], odd_post_g[i])
    return x
```

```python
import functools

import jax
import jax.numpy as jnp
from jax import lax
from jax.experimental import pallas as pl
from jax.experimental.pallas import tpu as pltpu

D_MODEL = 1024
N_MEM = 256
MIX_WIDTH = 2 * D_MODEL
XA_HEADS = 4
XA_WIDTH = MIX_WIDTH // 4
XA_HEAD_DIM = XA_WIDTH // XA_HEADS
BRANCH_WIDTH = (MIX_WIDTH - XA_WIDTH) // 2
CHUNK = 128
A_HEADS = 4
A_HEAD_DIM = BRANCH_WIDTH // A_HEADS
SHORT_CONV = 3
POOL_WINDOWS = (2, 4, 8, 16)
C_GROUP = BRANCH_WIDTH // len(POOL_WINDOWS)
CONF_CONV = 31
EPS = 1e-6

LANES = 128
SUBLANES = 8
N_SLABS = BRANCH_WIDTH // LANES
EVEN_TILE = 1024
ODD_TILE = 512
FENCE_ROWS = 2 * SUBLANES
VMEM_LIMIT_BYTES = 56 * 1024 * 1024

B_HALO = SUBLANES
C_HALO = 2 * SUBLANES
D_HALO = 4 * SUBLANES

BF16 = jnp.bfloat16
F32 = jnp.float32


def _dot(a, b):
    return jnp.dot(a, b, preferred_element_type=F32)


def _dot_pair(a, b):
    half = a.shape[0] // 2
    return jnp.concatenate([_dot(a[:half], b), _dot(a[half:], b)], axis=0)


def _sigmoid(x):
    return 0.5 * jnp.tanh(0.5 * x) + 0.5


def _silu(x):
    half = 0.5 * x
    return half * (jnp.tanh(half) + 1.0)


def _rms_norm(x, g):
    y = x * lax.rsqrt(jnp.mean(x * x, axis=-1, keepdims=True) + EPS)
    return y * g


def _layer_norm(x, g, b):
    mu = jnp.mean(x, axis=-1, keepdims=True)
    xc = x - mu
    var = jnp.mean(xc * xc, axis=-1, keepdims=True)
    return xc * lax.rsqrt(var + EPS) * g + b


def _slab(v, s):
    return v[:, s * LANES:(s + 1) * LANES]


def _cross_attention(q, kt_ref, v_ref):
    qb = q.astype(BF16)
    outs = []
    for hd in range(XA_HEADS):
        lo, hi = hd * XA_HEAD_DIM, (hd + 1) * XA_HEAD_DIM
        s = _dot(qb[:, lo:hi], kt_ref[0, lo:hi, :]) * (XA_HEAD_DIM ** -0.5)
        m = jnp.max(s, axis=-1, keepdims=True)
        e = jnp.exp(s - m)
        p = e / jnp.sum(e, axis=-1, keepdims=True)
        outs.append(_dot(p.astype(BF16), v_ref[0, :, lo:hi]))
    return jnp.concatenate(outs, axis=-1)


def _after(h, never_ref, value):
    rows, cols = value.shape
    folded = value.reshape(rows // FENCE_ROWS, FENCE_ROWS, cols).sum(axis=0)
    folded = sum(folded[:, c:c + LANES] for c in range(0, cols, LANES))
    first = jnp.where(never_ref[0] != 0, folded.astype(BF16), h[0:FENCE_ROWS, 0:LANES])
    top = jnp.concatenate([first, h[0:FENCE_ROWS, LANES:]], axis=1)
    return jnp.concatenate([top, h[FENCE_ROWS:]], axis=0)


def _carry_halo(ext_ref, halo, tile):
    for s in range(N_SLABS):
        ext_ref[s, 0:halo, :] = ext_ref[s, tile:tile + halo, :]


def _zero_halo(ext_ref, halo):
    ext_ref[:, 0:halo, :] = jnp.zeros((N_SLABS, halo, LANES), F32)


def _even_kernel(x_ref, kt_ref, v_ref, pre_g_ref, w_in_ref, ln_g_ref, ln_b_ref, ws_ref, bs_ref,
                 bconv_ref, w_out_ref, post_g_ref, o_ref, bext_ref):
    tile = x_ref.shape[1]
    bw = BRANCH_WIDTH

    @pl.when(pl.program_id(1) == 0)
    def _():
        _zero_halo(bext_ref, B_HALO)

    x = x_ref[0]
    h = _rms_norm(x, pre_g_ref[...]).astype(BF16)

    def proj(lo, width):
        return _dot(h, w_in_ref[:, lo:lo + width])

    u = proj(0, bw)
    v = proj(bw, bw)
    vn = _layer_norm(v, ln_g_ref[...], ln_b_ref[...]).astype(BF16)
    low_half = lax.broadcasted_iota(jnp.int32, (CHUNK, LANES), 1) < (LANES // 2)
    sg_chunks = []
    for c in range(tile // CHUNK):
        rows = vn[c * CHUNK:(c + 1) * CHUNK, :]
        r0 = _dot(ws_ref[0], rows[:, 0:256])
        r1 = _dot(ws_ref[1], rows[:, 128:384])
        r2 = _dot(ws_ref[2], rows[:, 384:640])
        r3 = _dot(ws_ref[3], rows[:, 512:768])
        sg_chunks.append(jnp.concatenate([
            r0[:, 0:128], jnp.where(low_half, r0[:, 128:256], r1[:, 0:128]), r1[:, 128:256],
            r2[:, 0:128], jnp.where(low_half, r2[:, 128:256], r3[:, 0:128]), r3[:, 128:256]],
            axis=-1) + bs_ref[...])
    ya = u * jnp.concatenate(sg_chunks, axis=0)

    bg = proj(2 * bw, bw)
    cx = proj(3 * bw, bw) * proj(4 * bw, bw)
    for s in range(N_SLABS):
        bext_ref[s, B_HALO:B_HALO + tile, :] = _slab(cx, s)
    conv = []
    for s in range(N_SLABS):
        acc = None
        for k in range(SHORT_CONV):
            term = bext_ref[s, pl.ds(B_HALO - (SHORT_CONV - 1) + k, tile), :] * bconv_ref[k:k + 1, s * LANES:(s + 1) * LANES]
            acc = term if acc is None else acc + term
        conv.append(acc)
    yb = bg * jnp.concatenate(conv, axis=-1)
    _carry_halo(bext_ref, B_HALO, tile)

    yx = _cross_attention(proj(5 * bw, XA_WIDTH), kt_ref, v_ref)

    g0 = 5 * bw + XA_WIDTH
    out = (_dot((ya * _silu(proj(g0, bw))).astype(BF16), w_out_ref[0:bw, :])
           + _dot((yb * _silu(proj(g0 + bw, bw))).astype(BF16), w_out_ref[bw:2 * bw, :])
           + _dot((yx * _silu(proj(g0 + 2 * bw, XA_WIDTH))).astype(BF16), w_out_ref[2 * bw:, :]))
    o_ref[0] = x + _rms_norm(out, post_g_ref[...])


def _window_sums(ext_ref, s, tile, windows):
    sums = {}
    acc = ext_ref[s, pl.ds(C_HALO, tile), :]
    for j in range(1, max(windows)):
        acc = acc + ext_ref[s, pl.ds(C_HALO - j, tile), :]
        if j + 1 in windows:
            sums[j + 1] = acc
    return sums


_SLAB_WINDOWS = ((2,), (2, 4), (4,), (8,), (8, 16), (16,))


def _odd_kernel(x_ref, kt_ref, v_ref, inv_cnt_ref, never_ref, pre_g_ref, w_in_ref, wgrp_ref, c_scale_ref,
                dw_w_ref, dw_b_ref, ln_g_ref, ln_b_ref, pw_w_ref, pw_b_ref, w_out_ref, post_g_ref,
                o_ref, cext_ref, zext_ref):
    tile = x_ref.shape[1]
    bw = BRANCH_WIDTH

    @pl.when(pl.program_id(1) == 0)
    def _():
        _zero_halo(cext_ref, C_HALO)
        _zero_halo(zext_ref, D_HALO)

    x = x_ref[0]
    h = _rms_norm(x, pre_g_ref[...]).astype(BF16)
    g0 = 3 * bw + XA_WIDTH

    def proj(lo, width, lhs=None):
        return _dot_pair(h if lhs is None else lhs, w_in_ref[:, lo:lo + width])

    z = proj(bw, bw) * _sigmoid(proj(2 * bw, bw))
    for s in range(N_SLABS):
        zext_ref[s, D_HALO:D_HALO + tile, :] = _slab(z, s)
    def conv_slab(s):
        acc = None
        for k in range(CONF_CONV):
            term = zext_ref[s, pl.ds(D_HALO - (CONF_CONV - 1) + k, tile), :] * dw_w_ref[k:k + 1, s * LANES:(s + 1) * LANES]
            acc = term if acc is None else acc + term
        return acc

    zc = proj(0, bw)
    q = proj(3 * bw, XA_WIDTH)
    conv = [conv_slab(0)]
    gate_c = proj(g0, bw, lhs=_after(h, never_ref, conv[0]))

    for s in range(N_SLABS):
        cext_ref[s, C_HALO:C_HALO + tile, :] = _slab(zc, s)
    low_half = lax.broadcasted_iota(jnp.int32, (tile, LANES), 1) < (LANES // 2)
    pooled = []
    for s in range(N_SLABS):
        sums = _window_sums(cext_ref, s, tile, _SLAB_WINDOWS[s])
        wins = _SLAB_WINDOWS[s]
        tot = sums[wins[0]] if len(wins) == 1 else jnp.where(low_half, sums[wins[0]], sums[wins[1]])
        inv = inv_cnt_ref[0, :, s * LANES:(s + 1) * LANES]
        mean = jnp.concatenate([tot[0:C_HALO] * inv, tot[C_HALO:] * inv[C_HALO - 1:C_HALO]], axis=0)
        pooled.append(mean - _slab(zc, s))
    _carry_halo(cext_ref, C_HALO, tile)
    pooled = jnp.concatenate(pooled, axis=-1).astype(BF16)
    half = bw // 2
    yc = jnp.concatenate([_dot(pooled[:, :half], wgrp_ref[0]), _dot(pooled[:, half:], wgrp_ref[1])],
                         axis=-1) * c_scale_ref[...]

    conv.append(conv_slab(1))
    gate_d = proj(g0 + bw, bw, lhs=_after(h, never_ref, conv[1]))

    yx = _cross_attention(q, kt_ref, v_ref)

    conv.append(conv_slab(2))
    gate_x = proj(g0 + 2 * bw, XA_WIDTH, lhs=_after(h, never_ref, conv[2]))

    conv.append(conv_slab(3))
    yc_g = (yc * _silu(gate_c)).astype(BF16)
    out = _dot_pair(_after(yc_g, never_ref, conv[3]), w_out_ref[0:bw, :])
    conv.append(conv_slab(4))
    yx_g = (yx * _silu(gate_x)).astype(BF16)
    out = out + _dot_pair(_after(yx_g, never_ref, conv[4]), w_out_ref[2 * bw:, :])
    conv.append(conv_slab(5))
    _carry_halo(zext_ref, D_HALO, tile)

    zd = jnp.concatenate(conv, axis=-1) + dw_b_ref[...]
    zd = _silu(_layer_norm(zd, ln_g_ref[...], ln_b_ref[...]))
    yd = _dot_pair(zd.astype(BF16), pw_w_ref[...]) + pw_b_ref[...]
    out = out + _dot_pair((yd * _silu(gate_d)).astype(BF16), w_out_ref[bw:2 * bw, :])
    o_ref[0] = x + _rms_norm(out, post_g_ref[...])


def _kv_kernel(mem_ref, g_ref, wkt_ref, wv_ref, kt_ref, v_ref):
    mem_n = _rms_norm(mem_ref[0], g_ref[...]).astype(BF16)
    kt = lax.dot_general(wkt_ref[...], mem_n, (((1,), (1,)), ((), ())), preferred_element_type=F32)
    kt_ref[0] = kt.astype(BF16)
    v_ref[0] = _dot(mem_n, wv_ref[...]).astype(BF16)


def _memory_kv(mem, mem_g, w_kv):
    bsz = mem.shape[0]
    wkt = w_kv[:, :XA_WIDTH].T.astype(BF16)
    wv = w_kv[:, XA_WIDTH:].astype(BF16)
    const = lambda b: (0, 0)
    return pl.pallas_call(
        _kv_kernel,
        grid=(bsz,),
        in_specs=[pl.BlockSpec((1, N_MEM, D_MODEL), lambda b: (b, 0, 0)),
                  pl.BlockSpec((1, D_MODEL), const),
                  pl.BlockSpec((XA_WIDTH, D_MODEL), const),
                  pl.BlockSpec((D_MODEL, XA_WIDTH), const)],
        out_specs=[pl.BlockSpec((1, XA_WIDTH, N_MEM), lambda b: (b, 0, 0)),
                   pl.BlockSpec((1, N_MEM, XA_WIDTH), lambda b: (b, 0, 0))],
        out_shape=[jax.ShapeDtypeStruct((bsz, XA_WIDTH, N_MEM), BF16),
                   jax.ShapeDtypeStruct((bsz, N_MEM, XA_WIDTH), BF16)],
        compiler_params=pltpu.CompilerParams(dimension_semantics=("arbitrary",)),
        name="memory_kv",
    )(mem, mem_g.reshape(1, D_MODEL), wkt, wv)


def _resident(shape):
    zeros = (0,) * len(shape)
    return pl.BlockSpec(shape, lambda b, j: zeros, pipeline_mode=pl.Buffered(1))


def _row(p):
    return p.reshape(1, -1)


def _layer_call(body, name, tile, x, kt, v, extra_specs, extra_args, resident_args, scratch_shapes):
    bsz, seq, d = x.shape
    assert seq % tile == 0 and tile % CHUNK == 0
    tile_spec = pl.BlockSpec((1, tile, d), lambda b, j: (b, j, 0))
    in_specs = ([tile_spec,
                 pl.BlockSpec((1, XA_WIDTH, N_MEM), lambda b, j: (b, 0, 0)),
                 pl.BlockSpec((1, N_MEM, XA_WIDTH), lambda b, j: (b, 0, 0))]
                + extra_specs + [_resident(a.shape) for a in resident_args])
    return pl.pallas_call(
        body,
        grid=(bsz, seq // tile),
        in_specs=in_specs,
        out_specs=tile_spec,
        out_shape=jax.ShapeDtypeStruct(x.shape, x.dtype),
        scratch_shapes=scratch_shapes,
        compiler_params=pltpu.CompilerParams(dimension_semantics=("arbitrary", "arbitrary"),
                                             vmem_limit_bytes=VMEM_LIMIT_BYTES),
        name=name,
    )(x, kt, v, *extra_args, *resident_args)


def _even_layer(x, mem, pre_g, w_in, a_ln_g, a_ln_b, a_ws, a_bs, b_conv, mem_g, w_kv, w_out, post_g):
    kt, v = _memory_kv(mem, mem_g, w_kv)
    causal = jnp.tril(jnp.ones((CHUNK, CHUNK), dtype=bool))
    ws = jnp.where(causal[None], a_ws, 0.0).astype(BF16)
    bs = jnp.repeat(a_bs.T, A_HEAD_DIM, axis=1)
    resident = [_row(pre_g), w_in.astype(BF16), _row(a_ln_g), _row(a_ln_b), ws, bs, b_conv,
                w_out.astype(BF16), _row(post_g)]
    scratch = [pltpu.VMEM((N_SLABS, B_HALO + EVEN_TILE, LANES), F32)]
    return _layer_call(_even_kernel, "even_layer", EVEN_TILE, x, kt, v, [], [], resident, scratch)


def _odd_layer(x, mem, pre_g, w_in, c_wgrp, c_scale, d_dw_w, d_dw_b, d_ln_g, d_ln_b, d_pw_w, d_pw_b,
               mem_g, w_kv, w_out, post_g):
    kt, v = _memory_kv(mem, mem_g, w_kv)
    tile = ODD_TILE
    win = jnp.repeat(jnp.asarray(POOL_WINDOWS, F32), C_GROUP)
    pos = jnp.arange(1, C_HALO + 1, dtype=F32)[:, None]
    assert max(POOL_WINDOWS) <= C_HALO
    inv_cnt = jnp.stack([1.0 / jnp.minimum(pos, win[None, :]),
                         jnp.broadcast_to(1.0 / win[None, :], (C_HALO, BRANCH_WIDTH))])
    zero = jnp.zeros((C_GROUP, C_GROUP), c_wgrp.dtype)
    wgrp = jnp.stack([jnp.block([[c_wgrp[0], zero], [zero, c_wgrp[1]]]),
                      jnp.block([[c_wgrp[2], zero], [zero, c_wgrp[3]]])]).astype(BF16)
    extra_specs = [pl.BlockSpec((1, C_HALO, BRANCH_WIDTH), lambda b, j: (jnp.minimum(j, 1), 0, 0)),
                   pl.BlockSpec(memory_space=pltpu.SMEM)]
    never = jnp.zeros((1,), jnp.int32)
    resident = [_row(pre_g), w_in.astype(BF16), wgrp, _row(c_scale), d_dw_w, _row(d_dw_b),
                _row(d_ln_g), _row(d_ln_b), d_pw_w.astype(BF16), _row(d_pw_b), w_out.astype(BF16),
                _row(post_g)]
    scratch = [pltpu.VMEM((N_SLABS, C_HALO + tile, LANES), F32),
               pltpu.VMEM((N_SLABS, D_HALO + tile, LANES), F32)]
    return _layer_call(_odd_kernel, "odd_layer", tile, x, kt, v, extra_specs, [inv_cnt, never], resident, scratch)


def kernel(x, mem, even_pre_g, even_w_in, even_a_ln_g, even_a_ln_b, even_a_ws, even_a_bs, even_b_conv, even_mem_g, even_w_kv, even_w_out, even_post_g, odd_pre_g, odd_w_in, odd_c_wgrp, odd_c_scale, odd_d_dw_w, odd_d_dw_b, odd_d_ln_g, odd_d_ln_b, odd_d_pw_w, odd_d_pw_b, odd_mem_g, odd_w_kv, odd_w_out, odd_post_g):
    depth = even_pre_g.shape[0] + odd_pre_g.shape[0]
    for layer in range(depth):
        i = layer // 2
        if layer % 2 == 0:
            x = _even_layer(x, mem, even_pre_g[i], even_w_in[i], even_a_ln_g[i], even_a_ln_b[i],
                            even_a_ws[i], even_a_bs[i], even_b_conv[i], even_mem_g[i], even_w_kv[i],
                            even_w_out[i], even_post_g[i])
        else:
            x = _odd_layer(x, mem, odd_pre_g[i], odd_w_in[i], odd_c_wgrp[i], odd_c_scale[i],
                           odd_d_dw_w[i], odd_d_dw_b[i], odd_d_ln_g[i], odd_d_ln_b[i], odd_d_pw_w[i],
                           odd_d_pw_b[i], odd_mem_g[i], odd_w_kv[i], odd_w_out[i], odd_post_g[i])
    return x
```

```python
import functools

import jax
import jax.numpy as jnp
from jax import lax
from jax.experimental import pallas as pl
from jax.experimental.pallas import tpu as pltpu

D_MODEL = 1024
N_MEM = 256
MIX_WIDTH = 2 * D_MODEL
XA_HEADS = 4
XA_WIDTH = MIX_WIDTH // 4
XA_HEAD_DIM = XA_WIDTH // XA_HEADS
BRANCH_WIDTH = (MIX_WIDTH - XA_WIDTH) // 2
CHUNK = 128
A_HEADS = 4
A_HEAD_DIM = BRANCH_WIDTH // A_HEADS
SHORT_CONV = 3
POOL_WINDOWS = (2, 4, 8, 16)
C_GROUP = BRANCH_WIDTH // len(POOL_WINDOWS)
CONF_CONV = 31
EPS = 1e-6

LANES = 128
SUBLANES = 8
N_SLABS = BRANCH_WIDTH // LANES
EVEN_TILE = 1024
ODD_TILE = 512
FENCE_ROWS = 2 * SUBLANES
VMEM_LIMIT_BYTES = 56 * 1024 * 1024

B_HALO = SUBLANES
C_HALO = 2 * SUBLANES
D_HALO = 4 * SUBLANES

BF16 = jnp.bfloat16
F32 = jnp.float32


def _dot(a, b):
    return jnp.dot(a, b, preferred_element_type=F32)


def _dot_pair(a, b):
    half = a.shape[0] // 2
    return jnp.concatenate([_dot(a[:half], b), _dot(a[half:], b)], axis=0)


def _sigmoid(x):
    return 0.5 * jnp.tanh(0.5 * x) + 0.5


def _silu(x):
    return x * _sigmoid(x)


def _rms_norm(x, g):
    y = x * lax.rsqrt(jnp.mean(x * x, axis=-1, keepdims=True) + EPS)
    return y * g


def _layer_norm(x, g, b):
    mu = jnp.mean(x, axis=-1, keepdims=True)
    xc = x - mu
    var = jnp.mean(xc * xc, axis=-1, keepdims=True)
    return xc * lax.rsqrt(var + EPS) * g + b


def _slab(v, s):
    return v[:, s * LANES:(s + 1) * LANES]


def _cross_attention(q, kt_ref, v_ref):
    qb = q.astype(BF16)
    outs = []
    for hd in range(XA_HEADS):
        lo, hi = hd * XA_HEAD_DIM, (hd + 1) * XA_HEAD_DIM
        s = _dot(qb[:, lo:hi], kt_ref[0, lo:hi, :]) * (XA_HEAD_DIM ** -0.5)
        m = jnp.max(s, axis=-1, keepdims=True)
        e = jnp.exp(s - m)
        p = e / jnp.sum(e, axis=-1, keepdims=True)
        outs.append(_dot(p.astype(BF16), v_ref[0, :, lo:hi]))
    return jnp.concatenate(outs, axis=-1)


def _after(h, never_ref, value):
    rows, cols = value.shape
    folded = value.reshape(rows // FENCE_ROWS, FENCE_ROWS, cols).sum(axis=0)
    folded = sum(folded[:, c:c + LANES] for c in range(0, cols, LANES))
    first = jnp.where(never_ref[0] != 0, folded.astype(BF16), h[0:FENCE_ROWS, 0:LANES])
    top = jnp.concatenate([first, h[0:FENCE_ROWS, LANES:]], axis=1)
    return jnp.concatenate([top, h[FENCE_ROWS:]], axis=0)


def _carry_halo(ext_ref, halo, tile):
    for s in range(N_SLABS):
        ext_ref[s, 0:halo, :] = ext_ref[s, tile:tile + halo, :]


def _zero_halo(ext_ref, halo):
    ext_ref[:, 0:halo, :] = jnp.zeros((N_SLABS, halo, LANES), F32)


def _even_kernel(x_ref, kt_ref, v_ref, pre_g_ref, w_in_ref, ln_g_ref, ln_b_ref, ws_ref, bs_ref,
                 bconv_ref, w_out_ref, post_g_ref, o_ref, bext_ref):
    tile = x_ref.shape[1]
    bw = BRANCH_WIDTH

    @pl.when(pl.program_id(1) == 0)
    def _():
        _zero_halo(bext_ref, B_HALO)

    x = x_ref[0]
    h = _rms_norm(x, pre_g_ref[...]).astype(BF16)

    def proj(lo, width):
        return _dot(h, w_in_ref[:, lo:lo + width])

    v = proj(bw, bw)
    u = proj(0, bw)
    bg = proj(2 * bw, bw)
    cg = proj(3 * bw, bw)
    xin = proj(4 * bw, bw)
    q = proj(5 * bw, XA_WIDTH)
    vn = _layer_norm(v, ln_g_ref[...], ln_b_ref[...]).astype(BF16)
    low_half = lax.broadcasted_iota(jnp.int32, (CHUNK, LANES), 1) < (LANES // 2)
    sg_chunks = []
    for c in range(tile // CHUNK):
        rows = vn[c * CHUNK:(c + 1) * CHUNK, :]
        r0 = _dot(ws_ref[0], rows[:, 0:256])
        r1 = _dot(ws_ref[1], rows[:, 128:384])
        r2 = _dot(ws_ref[2], rows[:, 384:640])
        r3 = _dot(ws_ref[3], rows[:, 512:768])
        sg_chunks.append(jnp.concatenate([
            r0[:, 0:128], jnp.where(low_half, r0[:, 128:256], r1[:, 0:128]), r1[:, 128:256],
            r2[:, 0:128], jnp.where(low_half, r2[:, 128:256], r3[:, 0:128]), r3[:, 128:256]],
            axis=-1) + bs_ref[...])
    ya = u * jnp.concatenate(sg_chunks, axis=0)

    cx = cg * xin
    for s in range(N_SLABS):
        bext_ref[s, B_HALO:B_HALO + tile, :] = _slab(cx, s)
    conv = []
    for s in range(N_SLABS):
        acc = None
        for k in range(SHORT_CONV):
            term = bext_ref[s, pl.ds(B_HALO - (SHORT_CONV - 1) + k, tile), :] * bconv_ref[k:k + 1, s * LANES:(s + 1) * LANES]
            acc = term if acc is None else acc + term
        conv.append(acc)
    yb = bg * jnp.concatenate(conv, axis=-1)
    _carry_halo(bext_ref, B_HALO, tile)

    yx = _cross_attention(q, kt_ref, v_ref)

    g0 = 5 * bw + XA_WIDTH
    out = (_dot((ya * _silu(proj(g0, bw))).astype(BF16), w_out_ref[0:bw, :])
           + _dot((yb * _silu(proj(g0 + bw, bw))).astype(BF16), w_out_ref[bw:2 * bw, :])
           + _dot((yx * _silu(proj(g0 + 2 * bw, XA_WIDTH))).astype(BF16), w_out_ref[2 * bw:, :]))
    o_ref[0] = x + _rms_norm(out, post_g_ref[...])


def _window_sums(ext_ref, s, tile, windows):
    sums = {}
    acc = ext_ref[s, pl.ds(C_HALO, tile), :]
    for j in range(1, max(windows)):
        acc = acc + ext_ref[s, pl.ds(C_HALO - j, tile), :]
        if j + 1 in windows:
            sums[j + 1] = acc
    return sums


_SLAB_WINDOWS = ((2,), (2, 4), (4,), (8,), (8, 16), (16,))


def _odd_kernel(x_ref, kt_ref, v_ref, inv_cnt_ref, never_ref, pre_g_ref, w_in_ref, wgrp_ref, c_scale_ref,
                dw_w_ref, dw_b_ref, ln_g_ref, ln_b_ref, pw_w_ref, pw_b_ref, w_out_ref, post_g_ref,
                o_ref, cext_ref, zext_ref):
    tile = x_ref.shape[1]
    bw = BRANCH_WIDTH

    @pl.when(pl.program_id(1) == 0)
    def _():
        _zero_halo(cext_ref, C_HALO)
        _zero_halo(zext_ref, D_HALO)

    x = x_ref[0]
    h = _rms_norm(x, pre_g_ref[...]).astype(BF16)
    g0 = 3 * bw + XA_WIDTH

    def proj(lo, width, lhs=None):
        return _dot_pair(h if lhs is None else lhs, w_in_ref[:, lo:lo + width])

    z = proj(bw, bw) * _sigmoid(proj(2 * bw, bw))
    for s in range(N_SLABS):
        zext_ref[s, D_HALO:D_HALO + tile, :] = _slab(z, s)
    def conv_slab(s):
        acc = None
        for k in range(CONF_CONV):
            term = zext_ref[s, pl.ds(D_HALO - (CONF_CONV - 1) + k, tile), :] * dw_w_ref[k:k + 1, s * LANES:(s + 1) * LANES]
            acc = term if acc is None else acc + term
        return acc

    zc = proj(0, bw)
    q = proj(3 * bw, XA_WIDTH)
    conv = [conv_slab(0)]
    gate_c = proj(g0, bw, lhs=_after(h, never_ref, conv[0]))

    for s in range(N_SLABS):
        cext_ref[s, C_HALO:C_HALO + tile, :] = _slab(zc, s)
    low_half = lax.broadcasted_iota(jnp.int32, (tile, LANES), 1) < (LANES // 2)
    pooled = []
    for s in range(N_SLABS):
        sums = _window_sums(cext_ref, s, tile, _SLAB_WINDOWS[s])
        wins = _SLAB_WINDOWS[s]
        tot = sums[wins[0]] if len(wins) == 1 else jnp.where(low_half, sums[wins[0]], sums[wins[1]])
        inv = inv_cnt_ref[0, :, s * LANES:(s + 1) * LANES]
        mean = jnp.concatenate([tot[0:C_HALO] * inv, tot[C_HALO:] * inv[C_HALO - 1:C_HALO]], axis=0)
        pooled.append(mean - _slab(zc, s))
    _carry_halo(cext_ref, C_HALO, tile)
    pooled = jnp.concatenate(pooled, axis=-1).astype(BF16)
    half = bw // 2
    yc = jnp.concatenate([_dot(pooled[:, :half], wgrp_ref[0]), _dot(pooled[:, half:], wgrp_ref[1])],
                         axis=-1) * c_scale_ref[...]

    conv.append(conv_slab(1))
    gate_d = proj(g0 + bw, bw, lhs=_after(h, never_ref, conv[1]))

    yx = _cross_attention(q, kt_ref, v_ref)

    conv.append(conv_slab(2))
    gate_x = proj(g0 + 2 * bw, XA_WIDTH, lhs=_after(h, never_ref, conv[2]))

    conv.append(conv_slab(3))
    yc_g = (yc * _silu(gate_c)).astype(BF16)
    out = _dot_pair(_after(yc_g, never_ref, conv[3]), w_out_ref[0:bw, :])
    conv.append(conv_slab(4))
    yx_g = (yx * _silu(gate_x)).astype(BF16)
    out = out + _dot_pair(_after(yx_g, never_ref, conv[4]), w_out_ref[2 * bw:, :])
    conv.append(conv_slab(5))
    _carry_halo(zext_ref, D_HALO, tile)

    zd = jnp.concatenate(conv, axis=-1) + dw_b_ref[...]
    zd = _silu(_layer_norm(zd, ln_g_ref[...], ln_b_ref[...]))
    yd = _dot_pair(zd.astype(BF16), pw_w_ref[...]) + pw_b_ref[...]
    out = out + _dot_pair((yd * _silu(gate_d)).astype(BF16), w_out_ref[bw:2 * bw, :])
    o_ref[0] = x + _rms_norm(out, post_g_ref[...])


def _kv_kernel(mem_ref, g_ref, wkt_ref, wv_ref, kt_ref, v_ref):
    mem_n = _rms_norm(mem_ref[0], g_ref[...]).astype(BF16)
    kt = lax.dot_general(wkt_ref[...], mem_n, (((1,), (1,)), ((), ())), preferred_element_type=F32)
    kt_ref[0] = kt.astype(BF16)
    v_ref[0] = _dot(mem_n, wv_ref[...]).astype(BF16)


def _memory_kv(mem, mem_g, w_kv):
    bsz = mem.shape[0]
    wkt = w_kv[:, :XA_WIDTH].T.astype(BF16)
    wv = w_kv[:, XA_WIDTH:].astype(BF16)
    const = lambda b: (0, 0)
    return pl.pallas_call(
        _kv_kernel,
        grid=(bsz,),
        in_specs=[pl.BlockSpec((1, N_MEM, D_MODEL), lambda b: (b, 0, 0)),
                  pl.BlockSpec((1, D_MODEL), const),
                  pl.BlockSpec((XA_WIDTH, D_MODEL), const),
                  pl.BlockSpec((D_MODEL, XA_WIDTH), const)],
        out_specs=[pl.BlockSpec((1, XA_WIDTH, N_MEM), lambda b: (b, 0, 0)),
                   pl.BlockSpec((1, N_MEM, XA_WIDTH), lambda b: (b, 0, 0))],
        out_shape=[jax.ShapeDtypeStruct((bsz, XA_WIDTH, N_MEM), BF16),
                   jax.ShapeDtypeStruct((bsz, N_MEM, XA_WIDTH), BF16)],
        compiler_params=pltpu.CompilerParams(dimension_semantics=("arbitrary",)),
        name="memory_kv",
    )(mem, mem_g.reshape(1, D_MODEL), wkt, wv)


def _resident(shape):
    zeros = (0,) * len(shape)
    return pl.BlockSpec(shape, lambda b, j: zeros, pipeline_mode=pl.Buffered(1))


def _row(p):
    return p.reshape(1, -1)


def _layer_call(body, name, tile, x, kt, v, extra_specs, extra_args, resident_args, scratch_shapes):
    bsz, seq, d = x.shape
    assert seq % tile == 0 and tile % CHUNK == 0
    tile_spec = pl.BlockSpec((1, tile, d), lambda b, j: (b, j, 0))
    in_specs = ([tile_spec,
                 pl.BlockSpec((1, XA_WIDTH, N_MEM), lambda b, j: (b, 0, 0)),
                 pl.BlockSpec((1, N_MEM, XA_WIDTH), lambda b, j: (b, 0, 0))]
                + extra_specs + [_resident(a.shape) for a in resident_args])
    return pl.pallas_call(
        body,
        grid=(bsz, seq // tile),
        in_specs=in_specs,
        out_specs=tile_spec,
        out_shape=jax.ShapeDtypeStruct(x.shape, x.dtype),
        scratch_shapes=scratch_shapes,
        compiler_params=pltpu.CompilerParams(dimension_semantics=("arbitrary", "arbitrary"),
                                             vmem_limit_bytes=VMEM_LIMIT_BYTES),
        name=name,
    )(x, kt, v, *extra_args, *resident_args)


def _even_layer(x, mem, pre_g, w_in, a_ln_g, a_ln_b, a_ws, a_bs, b_conv, mem_g, w_kv, w_out, post_g):
    kt, v = _memory_kv(mem, mem_g, w_kv)
    causal = jnp.tril(jnp.ones((CHUNK, CHUNK), dtype=bool))
    ws = jnp.where(causal[None], a_ws, 0.0).astype(BF16)
    bs = jnp.repeat(a_bs.T, A_HEAD_DIM, axis=1)
    resident = [_row(pre_g), w_in.astype(BF16), _row(a_ln_g), _row(a_ln_b), ws, bs, b_conv,
                w_out.astype(BF16), _row(post_g)]
    scratch = [pltpu.VMEM((N_SLABS, B_HALO + EVEN_TILE, LANES), F32)]
    return _layer_call(_even_kernel, "even_layer", EVEN_TILE, x, kt, v, [], [], resident, scratch)


def _odd_layer(x, mem, pre_g, w_in, c_wgrp, c_scale, d_dw_w, d_dw_b, d_ln_g, d_ln_b, d_pw_w, d_pw_b,
               mem_g, w_kv, w_out, post_g):
    kt, v = _memory_kv(mem, mem_g, w_kv)
    tile = ODD_TILE
    win = jnp.repeat(jnp.asarray(POOL_WINDOWS, F32), C_GROUP)
    pos = jnp.arange(1, C_HALO + 1, dtype=F32)[:, None]
    assert max(POOL_WINDOWS) <= C_HALO
    inv_cnt = jnp.stack([1.0 / jnp.minimum(pos, win[None, :]),
                         jnp.broadcast_to(1.0 / win[None, :], (C_HALO, BRANCH_WIDTH))])
    zero = jnp.zeros((C_GROUP, C_GROUP), c_wgrp.dtype)
    wgrp = jnp.stack([jnp.block([[c_wgrp[0], zero], [zero, c_wgrp[1]]]),
                      jnp.block([[c_wgrp[2], zero], [zero, c_wgrp[3]]])]).astype(BF16)
    extra_specs = [pl.BlockSpec((1, C_HALO, BRANCH_WIDTH), lambda b, j: (jnp.minimum(j, 1), 0, 0)),
                   pl.BlockSpec(memory_space=pltpu.SMEM)]
    never = jnp.zeros((1,), jnp.int32)
    resident = [_row(pre_g), w_in.astype(BF16), wgrp, _row(c_scale), d_dw_w, _row(d_dw_b),
                _row(d_ln_g), _row(d_ln_b), d_pw_w.astype(BF16), _row(d_pw_b), w_out.astype(BF16),
                _row(post_g)]
    scratch = [pltpu.VMEM((N_SLABS, C_HALO + tile, LANES), F32),
               pltpu.VMEM((N_SLABS, D_HALO + tile, LANES), F32)]
    return _layer_call(_odd_kernel, "odd_layer", tile, x, kt, v, extra_specs, [inv_cnt, never], resident, scratch)


def kernel(x, mem, even_pre_g, even_w_in, even_a_ln_g, even_a_ln_b, even_a_ws, even_a_bs, even_b_conv, even_mem_g, even_w_kv, even_w_out, even_post_g, odd_pre_g, odd_w_in, odd_c_wgrp, odd_c_scale, odd_d_dw_w, odd_d_dw_b, odd_d_ln_g, odd_d_ln_b, odd_d_pw_w, odd_d_pw_b, odd_mem_g, odd_w_kv, odd_w_out, odd_post_g):
    depth = even_pre_g.shape[0] + odd_pre_g.shape[0]
    for layer in range(depth):
        i = layer // 2
        if layer % 2 == 0:
            x = _even_layer(x, mem, even_pre_g[i], even_w_in[i], even_a_ln_g[i], even_a_ln_b[i],
                            even_a_ws[i], even_a_bs[i], even_b_conv[i], even_mem_g[i], even_w_kv[i],
                            even_w_out[i], even_post_g[i])
        else:
            x = _odd_layer(x, mem, odd_pre_g[i], odd_w_in[i], odd_c_wgrp[i], odd_c_scale[i],
                           odd_d_dw_w[i], odd_d_dw_b[i], odd_d_ln_g[i], odd_d_ln_b[i], odd_d_pw_w[i],
                           odd_d_pw_b[i], odd_mem_g[i], odd_w_kv[i], odd_w_out[i], odd_post_g[i])
    return x
```

```python
import functools

import jax
import jax.numpy as jnp
from jax import lax
from jax.experimental import pallas as pl
from jax.experimental.pallas import tpu as pltpu

D_MODEL = 1024
N_MEM = 256
MIX_WIDTH = 2 * D_MODEL
XA_HEADS = 4
XA_WIDTH = MIX_WIDTH // 4
XA_HEAD_DIM = XA_WIDTH // XA_HEADS
BRANCH_WIDTH = (MIX_WIDTH - XA_WIDTH) // 2
CHUNK = 128
A_HEADS = 4
A_HEAD_DIM = BRANCH_WIDTH // A_HEADS
SHORT_CONV = 3
POOL_WINDOWS = (2, 4, 8, 16)
C_GROUP = BRANCH_WIDTH // len(POOL_WINDOWS)
CONF_CONV = 31
EPS = 1e-6

LANES = 128
SUBLANES = 8
N_SLABS = BRANCH_WIDTH // LANES
EVEN_TILE = 1024
ODD_TILE = 512
FENCE_ROWS = 2 * SUBLANES
VMEM_LIMIT_BYTES = 56 * 1024 * 1024

B_HALO = SUBLANES
C_HALO = 2 * SUBLANES
D_HALO = 4 * SUBLANES

BF16 = jnp.bfloat16
F32 = jnp.float32


def _dot(a, b):
    return jnp.dot(a, b, preferred_element_type=F32)


def _dot_pair(a, b):
    half = a.shape[0] // 2
    return jnp.concatenate([_dot(a[:half], b), _dot(a[half:], b)], axis=0)


def _sigmoid(x):
    return 0.5 * jnp.tanh(0.5 * x) + 0.5


def _silu(x):
    return x * _sigmoid(x)


def _rms_norm(x, g):
    y = x * lax.rsqrt(jnp.mean(x * x, axis=-1, keepdims=True) + EPS)
    return y * g


def _layer_norm(x, g, b):
    mu = jnp.mean(x, axis=-1, keepdims=True)
    xc = x - mu
    var = jnp.mean(xc * xc, axis=-1, keepdims=True)
    return xc * lax.rsqrt(var + EPS) * g + b


def _slab(v, s):
    return v[:, s * LANES:(s + 1) * LANES]


def _cross_attention(q, kt_ref, v_ref):
    qb = q.astype(BF16)
    outs = []
    for hd in range(XA_HEADS):
        lo, hi = hd * XA_HEAD_DIM, (hd + 1) * XA_HEAD_DIM
        s = _dot(qb[:, lo:hi], kt_ref[0, lo:hi, :]) * (XA_HEAD_DIM ** -0.5)
        m = jnp.max(s, axis=-1, keepdims=True)
        e = jnp.exp(s - m)
        p = e / jnp.sum(e, axis=-1, keepdims=True)
        outs.append(_dot(p.astype(BF16), v_ref[0, :, lo:hi]))
    return jnp.concatenate(outs, axis=-1)


def _after(h, never_ref, value):
    rows, cols = value.shape
    folded = value.reshape(rows // FENCE_ROWS, FENCE_ROWS, cols).sum(axis=0)
    folded = sum(folded[:, c:c + LANES] for c in range(0, cols, LANES))
    first = jnp.where(never_ref[0] != 0, folded.astype(BF16), h[0:FENCE_ROWS, 0:LANES])
    top = jnp.concatenate([first, h[0:FENCE_ROWS, LANES:]], axis=1)
    return jnp.concatenate([top, h[FENCE_ROWS:]], axis=0)


def _carry_halo(ext_ref, halo, tile):
    for s in range(N_SLABS):
        ext_ref[s, 0:halo, :] = ext_ref[s, tile:tile + halo, :]


def _zero_halo(ext_ref, halo):
    ext_ref[:, 0:halo, :] = jnp.zeros((N_SLABS, halo, LANES), F32)


def _even_kernel(x_ref, mem_ref, mem_g_ref, wkt_ref, wv_ref, pre_g_ref, w_in_ref, ln_g_ref, ln_b_ref,
                 ws_ref, bs_ref, bconv_ref, w_out_ref, post_g_ref, o_ref, kt_ref, v_ref, bext_ref):
    tile = x_ref.shape[1]
    bw = BRANCH_WIDTH

    @pl.when(pl.program_id(1) == 0)
    def _():
        _zero_halo(bext_ref, B_HALO)
        _memory_kv(mem_ref, mem_g_ref, wkt_ref, wv_ref, kt_ref, v_ref)

    x = x_ref[0]
    h = _rms_norm(x, pre_g_ref[...]).astype(BF16)

    def proj(lo, width):
        return _dot(h, w_in_ref[:, lo:lo + width])

    v = proj(bw, bw)
    u = proj(0, bw)
    bg = proj(2 * bw, bw)
    cg = proj(3 * bw, bw)
    xin = proj(4 * bw, bw)
    q = proj(5 * bw, XA_WIDTH)
    vn = _layer_norm(v, ln_g_ref[...], ln_b_ref[...]).astype(BF16)
    low_half = lax.broadcasted_iota(jnp.int32, (CHUNK, LANES), 1) < (LANES // 2)
    sg_chunks = []
    for c in range(tile // CHUNK):
        rows = vn[c * CHUNK:(c + 1) * CHUNK, :]
        r0 = _dot(ws_ref[0], rows[:, 0:256])
        r1 = _dot(ws_ref[1], rows[:, 128:384])
        r2 = _dot(ws_ref[2], rows[:, 384:640])
        r3 = _dot(ws_ref[3], rows[:, 512:768])
        sg_chunks.append(jnp.concatenate([
            r0[:, 0:128], jnp.where(low_half, r0[:, 128:256], r1[:, 0:128]), r1[:, 128:256],
            r2[:, 0:128], jnp.where(low_half, r2[:, 128:256], r3[:, 0:128]), r3[:, 128:256]],
            axis=-1) + bs_ref[...])
    ya = u * jnp.concatenate(sg_chunks, axis=0)

    cx = cg * xin
    for s in range(N_SLABS):
        bext_ref[s, B_HALO:B_HALO + tile, :] = _slab(cx, s)
    conv = []
    for s in range(N_SLABS):
        acc = None
        for k in range(SHORT_CONV):
            term = bext_ref[s, pl.ds(B_HALO - (SHORT_CONV - 1) + k, tile), :] * bconv_ref[k:k + 1, s * LANES:(s + 1) * LANES]
            acc = term if acc is None else acc + term
        conv.append(acc)
    yb = bg * jnp.concatenate(conv, axis=-1)
    _carry_halo(bext_ref, B_HALO, tile)

    yx = _cross_attention(q, kt_ref, v_ref)

    g0 = 5 * bw + XA_WIDTH
    out = (_dot((ya * _silu(proj(g0, bw))).astype(BF16), w_out_ref[0:bw, :])
           + _dot((yb * _silu(proj(g0 + bw, bw))).astype(BF16), w_out_ref[bw:2 * bw, :])
           + _dot((yx * _silu(proj(g0 + 2 * bw, XA_WIDTH))).astype(BF16), w_out_ref[2 * bw:, :]))
    o_ref[0] = x + _rms_norm(out, post_g_ref[...])


def _window_sums(ext_ref, s, tile, windows):
    sums = {}
    acc = ext_ref[s, pl.ds(C_HALO, tile), :]
    for j in range(1, max(windows)):
        acc = acc + ext_ref[s, pl.ds(C_HALO - j, tile), :]
        if j + 1 in windows:
            sums[j + 1] = acc
    return sums


_SLAB_WINDOWS = ((2,), (2, 4), (4,), (8,), (8, 16), (16,))


def _odd_kernel(x_ref, mem_ref, mem_g_ref, wkt_ref, wv_ref, inv_cnt_ref, never_ref, pre_g_ref, w_in_ref,
                wgrp_ref, c_scale_ref, dw_w_ref, dw_b_ref, ln_g_ref, ln_b_ref, pw_w_ref, pw_b_ref,
                w_out_ref, post_g_ref, o_ref, kt_ref, v_ref, cext_ref, zext_ref):
    tile = x_ref.shape[1]
    bw = BRANCH_WIDTH

    @pl.when(pl.program_id(1) == 0)
    def _():
        _zero_halo(cext_ref, C_HALO)
        _zero_halo(zext_ref, D_HALO)
        _memory_kv(mem_ref, mem_g_ref, wkt_ref, wv_ref, kt_ref, v_ref)

    x = x_ref[0]
    h = _rms_norm(x, pre_g_ref[...]).astype(BF16)
    g0 = 3 * bw + XA_WIDTH

    def proj(lo, width, lhs=None):
        return _dot_pair(h if lhs is None else lhs, w_in_ref[:, lo:lo + width])

    z = proj(bw, bw) * _sigmoid(proj(2 * bw, bw))
    for s in range(N_SLABS):
        zext_ref[s, D_HALO:D_HALO + tile, :] = _slab(z, s)
    def conv_slab(s):
        acc = None
        for k in range(CONF_CONV):
            term = zext_ref[s, pl.ds(D_HALO - (CONF_CONV - 1) + k, tile), :] * dw_w_ref[k:k + 1, s * LANES:(s + 1) * LANES]
            acc = term if acc is None else acc + term
        return acc

    zc = proj(0, bw)
    q = proj(3 * bw, XA_WIDTH)
    conv = [conv_slab(0)]
    gate_c = proj(g0, bw, lhs=_after(h, never_ref, conv[0]))

    for s in range(N_SLABS):
        cext_ref[s, C_HALO:C_HALO + tile, :] = _slab(zc, s)
    low_half = lax.broadcasted_iota(jnp.int32, (tile, LANES), 1) < (LANES // 2)
    pooled = []
    for s in range(N_SLABS):
        sums = _window_sums(cext_ref, s, tile, _SLAB_WINDOWS[s])
        wins = _SLAB_WINDOWS[s]
        tot = sums[wins[0]] if len(wins) == 1 else jnp.where(low_half, sums[wins[0]], sums[wins[1]])
        inv = inv_cnt_ref[0, :, s * LANES:(s + 1) * LANES]
        mean = jnp.concatenate([tot[0:C_HALO] * inv, tot[C_HALO:] * inv[C_HALO - 1:C_HALO]], axis=0)
        pooled.append(mean - _slab(zc, s))
    _carry_halo(cext_ref, C_HALO, tile)
    pooled = jnp.concatenate(pooled, axis=-1).astype(BF16)
    half = bw // 2
    yc = jnp.concatenate([_dot(pooled[:, :half], wgrp_ref[0]), _dot(pooled[:, half:], wgrp_ref[1])],
                         axis=-1) * c_scale_ref[...]

    conv.append(conv_slab(1))
    gate_d = proj(g0 + bw, bw, lhs=_after(h, never_ref, conv[1]))

    yx = _cross_attention(q, kt_ref, v_ref)

    conv.append(conv_slab(2))
    gate_x = proj(g0 + 2 * bw, XA_WIDTH, lhs=_after(h, never_ref, conv[2]))

    conv.append(conv_slab(3))
    yc_g = (yc * _silu(gate_c)).astype(BF16)
    out = _dot_pair(_after(yc_g, never_ref, conv[3]), w_out_ref[0:bw, :])
    conv.append(conv_slab(4))
    yx_g = (yx * _silu(gate_x)).astype(BF16)
    out = out + _dot_pair(_after(yx_g, never_ref, conv[4]), w_out_ref[2 * bw:, :])
    conv.append(conv_slab(5))
    _carry_halo(zext_ref, D_HALO, tile)

    zd = jnp.concatenate(conv, axis=-1) + dw_b_ref[...]
    zd = _silu(_layer_norm(zd, ln_g_ref[...], ln_b_ref[...]))
    yd = _dot_pair(zd.astype(BF16), pw_w_ref[...]) + pw_b_ref[...]
    out = out + _dot_pair((yd * _silu(gate_d)).astype(BF16), w_out_ref[bw:2 * bw, :])
    o_ref[0] = x + _rms_norm(out, post_g_ref[...])


def _memory_kv(mem_ref, g_ref, wkt_ref, wv_ref, kt_ref, v_ref):
    mem_n = _rms_norm(mem_ref[0], g_ref[...]).astype(BF16)
    kt = lax.dot_general(wkt_ref[...], mem_n, (((1,), (1,)), ((), ())), preferred_element_type=F32)
    kt_ref[0] = kt.astype(BF16)
    v_ref[0] = _dot(mem_n, wv_ref[...]).astype(BF16)


def _resident(shape):
    zeros = (0,) * len(shape)
    return pl.BlockSpec(shape, lambda b, j: zeros, pipeline_mode=pl.Buffered(1))


def _row(p):
    return p.reshape(1, -1)


def _layer_call(body, name, tile, x, mem, mem_g, w_kv, extra_specs, extra_args, resident_args, scratch_shapes):
    bsz, seq, d = x.shape
    assert seq % tile == 0 and tile % CHUNK == 0
    tile_spec = pl.BlockSpec((1, tile, d), lambda b, j: (b, j, 0))
    kv_args = [_row(mem_g), w_kv[:, :XA_WIDTH].T.astype(BF16), w_kv[:, XA_WIDTH:].astype(BF16)]
    in_specs = ([tile_spec, pl.BlockSpec((1, N_MEM, d), lambda b, j: (b, 0, 0))]
                + [_resident(a.shape) for a in kv_args]
                + extra_specs + [_resident(a.shape) for a in resident_args])
    kv_scratch = [pltpu.VMEM((1, XA_WIDTH, N_MEM), BF16), pltpu.VMEM((1, N_MEM, XA_WIDTH), BF16)]
    return pl.pallas_call(
        body,
        grid=(bsz, seq // tile),
        in_specs=in_specs,
        out_specs=tile_spec,
        out_shape=jax.ShapeDtypeStruct(x.shape, x.dtype),
        scratch_shapes=kv_scratch + scratch_shapes,
        compiler_params=pltpu.CompilerParams(dimension_semantics=("arbitrary", "arbitrary"),
                                             vmem_limit_bytes=VMEM_LIMIT_BYTES),
        name=name,
    )(x, mem, *kv_args, *extra_args, *resident_args)


def _even_layer(x, mem, pre_g, w_in, a_ln_g, a_ln_b, a_ws, a_bs, b_conv, mem_g, w_kv, w_out, post_g):
    causal = jnp.tril(jnp.ones((CHUNK, CHUNK), dtype=bool))
    ws = jnp.where(causal[None], a_ws, 0.0).astype(BF16)
    bs = jnp.repeat(a_bs.T, A_HEAD_DIM, axis=1)
    resident = [_row(pre_g), w_in.astype(BF16), _row(a_ln_g), _row(a_ln_b), ws, bs, b_conv,
                w_out.astype(BF16), _row(post_g)]
    scratch = [pltpu.VMEM((N_SLABS, B_HALO + EVEN_TILE, LANES), F32)]
    return _layer_call(_even_kernel, "even_layer", EVEN_TILE, x, mem, mem_g, w_kv, [], [], resident, scratch)


def _odd_layer(x, mem, pre_g, w_in, c_wgrp, c_scale, d_dw_w, d_dw_b, d_ln_g, d_ln_b, d_pw_w, d_pw_b,
               mem_g, w_kv, w_out, post_g):
    tile = ODD_TILE
    win = jnp.repeat(jnp.asarray(POOL_WINDOWS, F32), C_GROUP)
    pos = jnp.arange(1, C_HALO + 1, dtype=F32)[:, None]
    assert max(POOL_WINDOWS) <= C_HALO
    inv_cnt = jnp.stack([1.0 / jnp.minimum(pos, win[None, :]),
                         jnp.broadcast_to(1.0 / win[None, :], (C_HALO, BRANCH_WIDTH))])
    zero = jnp.zeros((C_GROUP, C_GROUP), c_wgrp.dtype)
    wgrp = jnp.stack([jnp.block([[c_wgrp[0], zero], [zero, c_wgrp[1]]]),
                      jnp.block([[c_wgrp[2], zero], [zero, c_wgrp[3]]])]).astype(BF16)
    extra_specs = [pl.BlockSpec((1, C_HALO, BRANCH_WIDTH), lambda b, j: (jnp.minimum(j, 1), 0, 0)),
                   pl.BlockSpec(memory_space=pltpu.SMEM)]
    never = jnp.zeros((1,), jnp.int32)
    resident = [_row(pre_g), w_in.astype(BF16), wgrp, _row(c_scale), d_dw_w, _row(d_dw_b),
                _row(d_ln_g), _row(d_ln_b), d_pw_w.astype(BF16), _row(d_pw_b), w_out.astype(BF16),
                _row(post_g)]
    scratch = [pltpu.VMEM((N_SLABS, C_HALO + tile, LANES), F32),
               pltpu.VMEM((N_SLABS, D_HALO + tile, LANES), F32)]
    return _layer_call(_odd_kernel, "odd_layer", tile, x, mem, mem_g, w_kv, extra_specs, [inv_cnt, never],
                       resident, scratch)


def kernel(x, mem, even_pre_g, even_w_in, even_a_ln_g, even_a_ln_b, even_a_ws, even_a_bs, even_b_conv, even_mem_g, even_w_kv, even_w_out, even_post_g, odd_pre_g, odd_w_in, odd_c_wgrp, odd_c_scale, odd_d_dw_w, odd_d_dw_b, odd_d_ln_g, odd_d_ln_b, odd_d_pw_w, odd_d_pw_b, odd_mem_g, odd_w_kv, odd_w_out, odd_post_g):
    depth = even_pre_g.shape[0] + odd_pre_g.shape[0]
    for layer in range(depth):
        i = layer // 2
        if layer % 2 == 0:
            x = _even_layer(x, mem, even_pre_g[i], even_w_in[i], even_a_ln_g[i], even_a_ln_b[i],
                            even_a_ws[i], even_a_bs[i], even_b_conv[i], even_mem_g[i], even_w_kv[i],
                            even_w_out[i], even_post_g[i])
        else:
            x = _odd_layer(x, mem, odd_pre_g[i], odd_w_in[i], odd_c_wgrp[i], odd_c_scale[i],
                           odd_d_dw_w[i], odd_d_dw_b[i], odd_d_ln_g[i], odd_d_ln_b[i], odd_d_pw_w[i],
                           odd_d_pw_b[i], odd_mem_g[i], odd_w_kv[i], odd_w_out[i], odd_post_g[i])
    return x
```
